```python
import jax, jax.numpy as jnp
from jax import lax
import numpy as np

D_MODEL = 1024
BATCH = 8
SEQ = 4096
DEPTH = 4

HEAD_DIM = 64
N_HEADS_A = 8
N_HEADS_B = 8
N_HEADS_C = 8
WIDTH_A = N_HEADS_A * HEAD_DIM
WIDTH_B = N_HEADS_B * HEAD_DIM
WIDTH_C = N_HEADS_C * HEAD_DIM
WIDTH_D = D_MODEL // 2
POOL_WINDOWS = (2, 4, 8, 16)
N_POOL_GROUPS = len(POOL_WINDOWS)
POOL_GROUP_DIM = WIDTH_D // N_POOL_GROUPS
CHUNK = 128
Q_BLOCK = 128
PLE_DIM = 256
N_EXPERT_GROUPS = 4
EXPERTS_PER_GROUP = 4
N_EXPERTS = N_EXPERT_GROUPS * EXPERTS_PER_GROUP
TOP_K_INNER = 2
D_EXPERT = 256
RMS_EPS = 1e-6
N_EVEN = (DEPTH + 1) // 2
N_ODD = DEPTH // 2
EVEN_IN = 2 * WIDTH_A + 3 * WIDTH_B
ODD_IN = 3 * WIDTH_C + N_HEADS_C + WIDTH_D

kernel_name = "hybrid_sgu_stickbreak_fox_pool_hmoe"


def _rms_unit(x):
    xf = x.astype(jnp.float32)
    return (xf * lax.rsqrt(jnp.mean(xf * xf, axis=-1, keepdims=True) + RMS_EPS)).astype(x.dtype)


def rmsnorm(x, g):
    xf = x.astype(jnp.float32)
    y = xf * lax.rsqrt(jnp.mean(xf * xf, axis=-1, keepdims=True) + RMS_EPS)
    return (y * g.astype(jnp.float32)).astype(x.dtype)


def spatial_gating(u, v, w_s, b_s, g_v):
    bsz, seq, nh, hd = u.shape
    nc = seq // CHUNK
    u = jax.nn.gelu(u)
    v = rmsnorm(jax.nn.gelu(v), g_v)
    mask = jnp.tril(jnp.ones((CHUNK, CHUNK), dtype=bool))
    w = jnp.where(mask, w_s, 0.0)
    vc = v.reshape(bsz, nc, CHUNK, nh, hd)
    mixed = jnp.einsum('hts,bcshd->bcthd', w, vc) + b_s.T[None, None, :, :, None]
    return u * mixed.reshape(bsz, seq, nh, hd)


def stick_breaking_attention(q, k, v):
    bsz, seq, nh, hd = q.shape
    nb = seq // Q_BLOCK
    kf = k.astype(jnp.float32)
    vf = v.astype(jnp.float32)
    qb = q.astype(jnp.float32).reshape(bsz, nb, Q_BLOCK, nh, hd).swapaxes(0, 1)
    key_pos = jnp.arange(seq)
    scale = hd ** -0.5

    def block(args):
        q_blk, blk = args
        q_pos = blk * Q_BLOCK + jnp.arange(Q_BLOCK)
        strict = key_pos[None, :] < q_pos[:, None]
        z = jnp.einsum('bqhd,bkhd->bhqk', q_blk, kf) * scale
        log_1m_beta = jnp.where(strict, jax.nn.log_sigmoid(-z), 0.0)
        between = lax.cumsum(log_1m_beta, axis=3, reverse=True) - log_1m_beta
        weights = jnp.where(strict, jnp.exp(jax.nn.log_sigmoid(z) + between), 0.0)
        return jnp.einsum('bhqk,bkhd->bqhd', weights, vf)

    out = lax.map(block, (qb, jnp.arange(nb)))
    return out.swapaxes(0, 1).reshape(bsz, seq, nh, hd).astype(q.dtype)


def forgetting_attention(q, k, v, log_f):
    bsz, seq, nh, hd = q.shape
    nb = seq // Q_BLOCK
    cum_f = jnp.cumsum(log_f.astype(jnp.float32), axis=1)
    f_key = cum_f.transpose(0, 2, 1)
    f_qb = cum_f.reshape(bsz, nb, Q_BLOCK, nh).swapaxes(0, 1)
    kf = k.astype(jnp.float32)
    vf = v.astype(jnp.float32)
    qb = q.astype(jnp.float32).reshape(bsz, nb, Q_BLOCK, nh, hd).swapaxes(0, 1)
    key_pos = jnp.arange(seq)
    scale = hd ** -0.5

    def block(args):
        q_blk, fq_blk, blk = args
        q_pos = blk * Q_BLOCK + jnp.arange(Q_BLOCK)
        causal = key_pos[None, :] <= q_pos[:, None]
        z = (jnp.einsum('bqhd,bkhd->bhqk', q_blk, kf) * scale
             + fq_blk.transpose(0, 2, 1)[..., None] - f_key[:, :, None, :])
        probs = jax.nn.softmax(jnp.where(causal, z, -jnp.inf), axis=-1)
        return jnp.einsum('bhqk,bkhd->bqhd', probs, vf)

    out = lax.map(block, (qb, f_qb, jnp.arange(nb)))
    return out.swapaxes(0, 1).reshape(bsz, seq, nh, hd).astype(q.dtype)


def multiscale_pool(xd, w_pool, pool_scale):
    bsz, seq, _ = xd.shape
    xg = xd.astype(jnp.float32).reshape(bsz, seq, N_POOL_GROUPS, POOL_GROUP_DIM)
    cs = jnp.concatenate([jnp.zeros((bsz, 1, N_POOL_GROUPS, POOL_GROUP_DIM), jnp.float32),
                          jnp.cumsum(xg, axis=1)], axis=1)
    pos = jnp.arange(seq)
    outs = []
    for g, win in enumerate(POOL_WINDOWS):
        csg = cs[:, :, g]
        start = jnp.maximum(pos + 1 - win, 0)
        count = (pos + 1 - start).astype(jnp.float32)
        mean = (csg[:, 1:] - csg[:, start]) / count[None, :, None]
        outs.append(mean - xg[:, :, g])
    pooled = jnp.stack(outs, axis=2)
    mixed = jnp.einsum('bsgc,gcd->bsgd', pooled, w_pool.astype(jnp.float32))
    return (mixed.reshape(bsz, seq, WIDTH_D) * pool_scale.astype(jnp.float32)).astype(xd.dtype)


def even_mixer(h, w_in, sgu_w, sgu_b, sgu_norm, w_out):
    bsz, seq, _ = h.shape
    proj = h @ w_in
    u, va, q, k, vb = jnp.split(
        proj, [WIDTH_A, 2 * WIDTH_A, 2 * WIDTH_A + WIDTH_B, 2 * WIDTH_A + 2 * WIDTH_B], axis=-1)
    heads = lambda t, n: t.reshape(bsz, seq, n, HEAD_DIM)
    ya = spatial_gating(heads(u, N_HEADS_A), heads(va, N_HEADS_A), sgu_w, sgu_b, sgu_norm)
    yb = stick_breaking_attention(heads(q, N_HEADS_B), heads(k, N_HEADS_B), heads(vb, N_HEADS_B))
    y = jnp.concatenate([ya.reshape(bsz, seq, WIDTH_A), yb.reshape(bsz, seq, WIDTH_B)], axis=-1)
    return y @ w_out


def odd_mixer(h, w_in, forget_bias, q_norm, k_norm, pool_w, pool_scale, w_out):
    bsz, seq, _ = h.shape
    proj = h @ w_in
    q, k, v, f_logit, xd = jnp.split(
        proj, [WIDTH_C, 2 * WIDTH_C, 3 * WIDTH_C, 3 * WIDTH_C + N_HEADS_C], axis=-1)
    heads = lambda t: t.reshape(bsz, seq, N_HEADS_C, HEAD_DIM)
    q = rmsnorm(heads(q), q_norm)
    k = rmsnorm(heads(k), k_norm)
    log_f = jax.nn.log_sigmoid(f_logit.astype(jnp.float32) + forget_bias.astype(jnp.float32))
    yc = forgetting_attention(q, k, heads(v), log_f)
    yd = multiscale_pool(xd, pool_w, pool_scale)
    y = jnp.concatenate([yc.reshape(bsz, seq, WIDTH_C), yd], axis=-1)
    return y @ w_out


def hierarchical_moe(h, rg_w, rg_b, re_w, re_b, w_gate, w_up, w_down):
    def one_sequence(t):
        n_tok = t.shape[0]
        group_prob = jax.nn.softmax((t @ rg_w).astype(jnp.float32) + rg_b.astype(jnp.float32), axis=-1)
        g_p, g_idx = lax.top_k(group_prob, 1)
        e_logits = ((t @ re_w).astype(jnp.float32) + re_b.astype(jnp.float32)).reshape(
            n_tok, N_EXPERT_GROUPS, EXPERTS_PER_GROUP)
        in_group = jnp.take_along_axis(e_logits, g_idx[:, :, None], axis=1)[:, 0]
        e_top, e_idx = lax.top_k(in_group, TOP_K_INNER)
        e_w = jax.nn.softmax(e_top, axis=-1) * g_p
        flat_idx = g_idx * EXPERTS_PER_GROUP + e_idx
        gates = jnp.sum(jax.nn.one_hot(flat_idx, N_EXPERTS, dtype=jnp.float32) * e_w[..., None], axis=1)
        hg = jnp.einsum('sd,edf->sef', t, w_gate)
        hu = jnp.einsum('sd,edf->sef', t, w_up)
        act = jax.nn.silu(hg) * hu * gates[:, :, None].astype(t.dtype)
        return jnp.einsum('sef,efd->sd', act, w_down)

    return lax.map(one_sequence, h)


def per_layer_embedding(h, p_i, w_proj, g_norm, w_gate):
    e = rmsnorm(p_i @ w_proj, g_norm)
    gate = jax.nn.sigmoid(_rms_unit(h) @ w_gate)
    return h + gate * e


def setup_inputs(seed: int = 0) -> dict:
    key = jax.random.key(seed)
    ks = iter(jax.random.split(key, 32))
    nrm = lambda shape, scale: jax.random.normal(next(ks), shape, jnp.float32) * scale
    gain = lambda shape: 1.0 + 0.02 * jax.random.normal(next(ks), shape, jnp.float32)
    return {
        "x": nrm((BATCH, SEQ, D_MODEL), 1.0),
        "p": nrm((DEPTH, BATCH, SEQ, PLE_DIM), 1.0),
        "norm_mix": gain((DEPTH, D_MODEL)),
        "norm_ffn": gain((DEPTH, D_MODEL)),
        "even_w_in": nrm((N_EVEN, D_MODEL, EVEN_IN), D_MODEL ** -0.5),
        "sgu_w": nrm((N_EVEN, N_HEADS_A, CHUNK, CHUNK), CHUNK ** -0.5),
        "sgu_b": gain((N_EVEN, N_HEADS_A, CHUNK)),
        "sgu_norm": gain((N_EVEN, N_HEADS_A, HEAD_DIM)),
        "even_w_out": nrm((N_EVEN, WIDTH_A + WIDTH_B, D_MODEL), (WIDTH_A + WIDTH_B) ** -0.5),
        "odd_w_in": nrm((N_ODD, D_MODEL, ODD_IN), D_MODEL ** -0.5),
        "forget_bias": jax.random.uniform(next(ks), (N_ODD, N_HEADS_C), jnp.float32, 1.0, 4.0),
        "q_norm": gain((N_ODD, HEAD_DIM)),
        "k_norm": gain((N_ODD, HEAD_DIM)),
        "pool_w": nrm((N_ODD, N_POOL_GROUPS, POOL_GROUP_DIM, POOL_GROUP_DIM), POOL_GROUP_DIM ** -0.5),
        "pool_scale": 1.0 + 0.1 * jax.random.normal(next(ks), (N_ODD, WIDTH_D), jnp.float32),
        "odd_w_out": nrm((N_ODD, WIDTH_C + WIDTH_D, D_MODEL), (WIDTH_C + WIDTH_D) ** -0.5),
        "router_group_w": nrm((DEPTH, D_MODEL, N_EXPERT_GROUPS), D_MODEL ** -0.5),
        "router_group_b": nrm((DEPTH, N_EXPERT_GROUPS), 0.01),
        "router_expert_w": nrm((DEPTH, D_MODEL, N_EXPERTS), D_MODEL ** -0.5),
        "router_expert_b": nrm((DEPTH, N_EXPERTS), 0.01),
        "expert_w_gate": nrm((DEPTH, N_EXPERTS, D_MODEL, D_EXPERT), D_MODEL ** -0.5),
        "expert_w_up": nrm((DEPTH, N_EXPERTS, D_MODEL, D_EXPERT), D_MODEL ** -0.5),
        "expert_w_down": nrm((DEPTH, N_EXPERTS, D_EXPERT, D_MODEL), D_EXPERT ** -0.5),
        "ple_w_proj": nrm((DEPTH, PLE_DIM, D_MODEL), PLE_DIM ** -0.5),
        "ple_norm": gain((DEPTH, D_MODEL)),
        "ple_w_gate": nrm((DEPTH, D_MODEL, D_MODEL), D_MODEL ** -0.5),
    }


def reference(x, p, norm_mix, norm_ffn, even_w_in, sgu_w, sgu_b, sgu_norm, even_w_out,
              odd_w_in, forget_bias, q_norm, k_norm, pool_w, pool_scale, odd_w_out,
              router_group_w, router_group_b, router_expert_w, router_expert_b,
              expert_w_gate, expert_w_up, expert_w_down, ple_w_proj, ple_norm, ple_w_gate):
    h = x
    for i in range(DEPTH):
        j = i // 2
        hn = rmsnorm(h, norm_mix[i])
        if i % 2 == 0:
            h = h + even_mixer(hn, even_w_in[j], sgu_w[j], sgu_b[j], sgu_norm[j], even_w_out[j])
        else:
            h = h + odd_mixer(hn, odd_w_in[j], forget_bias[j], q_norm[j], k_norm[j],
                              pool_w[j], pool_scale[j], odd_w_out[j])
        h = h + hierarchical_moe(rmsnorm(h, norm_ffn[i]), router_group_w[i], router_group_b[i],
                                 router_expert_w[i], router_expert_b[i],
                                 expert_w_gate[i], expert_w_up[i], expert_w_down[i])
        h = per_layer_embedding(h, p[i], ple_w_proj[i], ple_norm[i], ple_w_gate[i])
    return h
```

```python
import functools

import jax
import jax.numpy as jnp
from jax import lax
from jax.experimental import pallas as pl
from jax.experimental.pallas import tpu as pltpu

F32 = jnp.float32
BF16 = jnp.bfloat16

HEAD_DIM = 64
LANES = 128
CHUNK = 128
POOL_WINDOWS = (2, 4, 8, 16)
MAX_WINDOW = max(POOL_WINDOWS)
N_GROUPS = 4
EXPERTS_PER_GROUP = 4
N_EXPERTS = N_GROUPS * EXPERTS_PER_GROUP
D_EXPERT = 256
RMS_EPS = 1e-6
ATT_BLOCK = 256
ROW_TILE = 512
MOE_TILE = 1024
VMEM_LIMIT = 48 * 1024 * 1024
EXP_ZERO_BELOW = -104.0
NEG_BIG = -1e30


def _params(*sem):
    return pltpu.CompilerParams(dimension_semantics=sem, vmem_limit_bytes=VMEM_LIMIT)


def _split_bf16(x):
    hi = x.astype(BF16)
    lo = (x - hi.astype(F32)).astype(BF16)
    return hi, lo


def _dot(a, b):
    return jnp.dot(a, b, preferred_element_type=F32)


def _dot_nt(a, b):
    return lax.dot_general(a, b, (((1,), (1,)), ((), ())), preferred_element_type=F32)


def _dot_exact_rhs(x, m):
    hi, lo = _split_bf16(x)
    return _dot(hi, m) + _dot(lo, m)


def _dot3(x, w_hi, w_lo):
    hi, lo = _split_bf16(x)
    return _dot(hi, w_hi) + (_dot(hi, w_lo) + _dot(lo, w_hi))


def _dot3_nt(w_hi, w_lo, x):
    hi, lo = _split_bf16(x)
    return _dot_nt(w_hi, hi) + (_dot_nt(w_lo, hi) + _dot_nt(w_hi, lo))


def _rms_normalize(x):
    return x * lax.rsqrt(jnp.mean(x * x, axis=-1, keepdims=True) + RMS_EPS)


def _log_sigmoid(x):
    return jnp.minimum(x, 0.0) - jnp.log1p(jnp.exp(-jnp.abs(x)))


def _gelu_tanh(x):
    return 0.5 * x * (1.0 + jnp.tanh(0.7978845608028654 * (x + 0.044715 * (x * x * x))))


def _head_group_matrix():
    r = lax.broadcasted_iota(jnp.int32, (LANES, LANES), 0) // HEAD_DIM
    c = lax.broadcasted_iota(jnp.int32, (LANES, LANES), 1) // HEAD_DIM
    return jnp.where(r == c, 1.0, 0.0).astype(BF16)


def _first_head_mask(shape):
    return lax.broadcasted_iota(jnp.int32, shape, len(shape) - 1) < HEAD_DIM


def _even_in_kernel(h_ref, g_ref, w_ref, u_ref, va_ref, q_ref, k_ref, vb_ref, *, width):
    hn = (_rms_normalize(h_ref[...]) * g_ref[...]).astype(BF16)
    outs = (u_ref, va_ref, q_ref, k_ref, vb_ref)
    for c, o in enumerate(outs):
        r = _dot(hn, w_ref[:, c * width:(c + 1) * width])
        if o is q_ref:
            r = r * (HEAD_DIM ** -0.5)
        o[...] = r.astype(o.dtype)


def _even_in_proj(h, g, w):
    m, d = h.shape
    width = w.shape[1] // 5
    out = jax.ShapeDtypeStruct((m, width), BF16)
    row = pl.BlockSpec((ROW_TILE, width), lambda i: (i, 0))
    return pl.pallas_call(
        functools.partial(_even_in_kernel, width=width),
        grid=(m // ROW_TILE,),
        in_specs=[pl.BlockSpec((ROW_TILE, d), lambda i: (i, 0)),
                  pl.BlockSpec((1, d), lambda i: (0, 0)),
                  pl.BlockSpec(w.shape, lambda i: (0, 0))],
        out_specs=[row] * 5,
        out_shape=[out] * 5,
        compiler_params=_params("parallel"),
        name="even_in_proj",
    )(h, g, w)


def _sgu_kernel(u_ref, v_ref, w_ref, b_ref, g_ref, o_ref):
    n_pairs = u_ref.shape[1] // LANES
    n_chunks = u_ref.shape[0] // CHUNK
    gmat = _head_group_matrix()
    first = _first_head_mask((CHUNK, LANES))
    tril = (lax.broadcasted_iota(jnp.int32, (CHUNK, CHUNK), 0)
            >= lax.broadcasted_iota(jnp.int32, (CHUNK, CHUNK), 1))
    for p in range(n_pairs):
        cols = slice(p * LANES, (p + 1) * LANES)
        w_a = jnp.where(tril, w_ref[2 * p], 0.0).astype(BF16)
        w_b = jnp.where(tril, w_ref[2 * p + 1], 0.0).astype(BF16)
        v = _gelu_tanh(v_ref[:, cols].astype(F32))
        ms = _dot_exact_rhs(v * v, gmat) * (1.0 / HEAD_DIM)
        vn = v * lax.rsqrt(ms + RMS_EPS) * g_ref[:, cols]
        for c in range(n_chunks):
            rows = slice(c * CHUNK, (c + 1) * CHUNK)
            vc = vn[rows]
            mixed = (_dot(w_a, jnp.where(first, vc, 0.0).astype(BF16))
                     + _dot(w_b, jnp.where(first, 0.0, vc).astype(BF16)) + b_ref[:, cols])
            u = _gelu_tanh(u_ref[rows, cols].astype(F32))
            o_ref[rows, cols] = (u * mixed).astype(o_ref.dtype)


def _spatial_gating(u, va, w_s, bias, g_v):
    m, width = u.shape
    row = pl.BlockSpec((ROW_TILE, width), lambda i: (i, 0))
    return pl.pallas_call(
        _sgu_kernel,
        grid=(m // ROW_TILE,),
        in_specs=[row, row,
                  pl.BlockSpec(w_s.shape, lambda i: (0, 0, 0)),
                  pl.BlockSpec(bias.shape, lambda i: (0, 0)),
                  pl.BlockSpec(g_v.shape, lambda i: (0, 0))],
        out_specs=row,
        out_shape=jax.ShapeDtypeStruct((m, width), BF16),
        compiler_params=_params("parallel"),
        name="spatial_gating",
    )(u, va, w_s, bias, g_v)


def _stick_kernel(q_ref, k_ref, v_ref, o_ref, acc_ref, r_ref):
    blk = ATT_BLOCK
    i = pl.program_id(2)
    first_q = _first_head_mask((blk, LANES))
    q = q_ref[...].astype(F32)
    q_heads = (jnp.where(first_q, q, 0.0).astype(BF16), jnp.where(first_q, 0.0, q).astype(BF16))
    row = lax.broadcasted_iota(jnp.int32, (blk, blk), 0)
    col = lax.broadcasted_iota(jnp.int32, (blk, blk), 1)
    suffix = jnp.where(row >= col, 1.0, 0.0).astype(BF16)
    strict = col < row
    acc_ref[...] = jnp.zeros_like(acc_ref)
    r_ref[...] = jnp.zeros_like(r_ref)

    def sweep(j, diagonal):
        start = pl.multiple_of(j * blk, blk)
        k = k_ref[pl.ds(start, blk), :]
        v = v_ref[pl.ds(start, blk), :].astype(F32)
        v_heads = (jnp.where(first_q, v, 0.0).astype(BF16), jnp.where(first_q, 0.0, v).astype(BF16))
        acc = acc_ref[...]
        for hd in range(2):
            z = _dot_nt(q_heads[hd], k)
            log_1m_beta = jnp.minimum(-z, 0.0) - jnp.log1p(jnp.exp(-jnp.abs(z)))
            if diagonal:
                log_1m_beta = jnp.where(strict, log_1m_beta, 0.0)
            tail = _dot_exact_rhs(log_1m_beta, suffix)
            r = r_ref[hd]
            weights = jnp.exp(z + tail + jnp.concatenate([r] * (blk // LANES), axis=1))
            if diagonal:
                weights = jnp.where(strict, weights, 0.0)
            acc = acc + _dot(weights.astype(BF16), v_heads[hd])
            r_ref[hd] = r + jnp.broadcast_to(tail[:, 0:1], r.shape)
        acc_ref[...] = acc

    sweep(i, True)

    def cond(carry):
        j, live = carry
        return jnp.logical_and(j >= 0, live)

    def any_live():
        return jnp.max(jnp.maximum(r_ref[0], r_ref[1])) > EXP_ZERO_BELOW

    def body(carry):
        j, _ = carry
        sweep(j, False)
        return j - 1, any_live()

    lax.while_loop(cond, body, (i - 1, any_live()))
    o_ref[...] = acc_ref[...].astype(o_ref.dtype)


def _stick_breaking(q, k, v, batch, seq):
    m, width = q.shape
    nq = seq // ATT_BLOCK
    n_pairs = width // LANES
    qspec = pl.BlockSpec((ATT_BLOCK, LANES), lambda b, p, i: (b * nq + i, p))
    kvspec = pl.BlockSpec((seq, LANES), lambda b, p, i: (b, p))
    return pl.pallas_call(
        _stick_kernel,
        grid=(batch, n_pairs, nq),
        in_specs=[qspec, kvspec, kvspec],
        out_specs=qspec,
        out_shape=jax.ShapeDtypeStruct((m, width), BF16),
        scratch_shapes=[pltpu.VMEM((ATT_BLOCK, LANES), F32),
                        pltpu.VMEM((2, ATT_BLOCK, LANES), F32)],
        compiler_params=_params("parallel", "parallel", "arbitrary"),
        name="stick_breaking",
    )(q, k, v)


def _odd_in_kernel(h_ref, g_ref, w_ref, wf_hi_ref, wf_lo_ref, fb_ref, qn_ref, kn_ref,
                   q_ref, k_ref, v_ref, xd_ref, nf_ref, carry_ref, *, width, tiles_per_seq):
    i = pl.program_id(0)
    hn32 = _rms_normalize(h_ref[...]) * g_ref[...]
    hn = hn32.astype(BF16)
    gmat = _head_group_matrix()

    def head_norm(x, gain_ref):
        parts = []
        for p in range(width // LANES):
            xp = x[:, p * LANES:(p + 1) * LANES]
            ms = _dot_exact_rhs(xp * xp, gmat) * (1.0 / HEAD_DIM)
            parts.append(xp * lax.rsqrt(ms + RMS_EPS) * gain_ref[...])
        return jnp.concatenate(parts, axis=1)

    qf = head_norm(_dot(hn, w_ref[:, 0:width]), qn_ref) * (HEAD_DIM ** -0.5)
    q_ref[...] = qf.astype(q_ref.dtype)
    k_ref[...] = head_norm(_dot(hn, w_ref[:, width:2 * width]), kn_ref).astype(k_ref.dtype)
    v_ref[...] = _dot(hn, w_ref[:, 2 * width:3 * width]).astype(v_ref.dtype)
    xd_ref[...] = _dot(hn, w_ref[:, 3 * width:4 * width]).astype(xd_ref.dtype)

    f_logit = _dot3_nt(wf_hi_ref[...], wf_lo_ref[...], hn32)
    log_f = _log_sigmoid(f_logit + fb_ref[...])
    n_heads, tile = log_f.shape

    @pl.when(i % tiles_per_seq == 0)
    def _():
        carry_ref[...] = jnp.zeros_like(carry_ref)

    r = lax.broadcasted_iota(jnp.int32, (LANES, LANES), 0)
    c = lax.broadcasted_iota(jnp.int32, (LANES, LANES), 1)
    prefix = jnp.where(r <= c, 1.0, 0.0).astype(BF16)
    carry = carry_ref[...]
    for s in range(tile // LANES):
        x = log_f[:, s * LANES:(s + 1) * LANES]
        x1 = x.astype(BF16)
        rem = x - x1.astype(F32)
        x2 = rem.astype(BF16)
        x3 = (rem - x2.astype(F32)).astype(BF16)
        cum = _dot(x1, prefix) + (_dot(x2, prefix) + _dot(x3, prefix)) + carry
        nf_ref[:, s * LANES:(s + 1) * LANES] = -cum
        carry = jnp.broadcast_to(cum[:, LANES - 1:LANES], carry.shape)
    carry_ref[...] = carry


def _odd_in_proj(h, g, w, wf_hi, wf_lo, fb, qn, kn, seq):
    m, d = h.shape
    width = w.shape[1] // 4
    n_heads = wf_hi.shape[0]
    row = lambda dt: pl.BlockSpec((ROW_TILE, width), lambda i: (i, 0))
    full = lambda a: pl.BlockSpec(a.shape, lambda i: (0,) * a.ndim)
    return pl.pallas_call(
        functools.partial(_odd_in_kernel, width=width, tiles_per_seq=seq // ROW_TILE),
        grid=(m // ROW_TILE,),
        in_specs=[pl.BlockSpec((ROW_TILE, d), lambda i: (i, 0)),
                  full(g), full(w), full(wf_hi), full(wf_lo), full(fb), full(qn), full(kn)],
        out_specs=[row(BF16), row(BF16), row(BF16), row(F32),
                   pl.BlockSpec((n_heads, ROW_TILE), lambda i: (0, i))],
        out_shape=[jax.ShapeDtypeStruct((m, width), BF16)] * 3
                  + [jax.ShapeDtypeStruct((m, width), F32),
                     jax.ShapeDtypeStruct((n_heads, m), F32)],
        scratch_shapes=[pltpu.VMEM((n_heads, LANES), F32)],
        compiler_params=_params("arbitrary"),
        name="odd_in_proj",
    )(h, g, w, wf_hi, wf_lo, fb, qn, kn)


def _forget_kernel(q_ref, k_ref, v_ref, nf_ref, o_ref, acc_ref, m_ref):
    blk = ATT_BLOCK
    i = pl.program_id(2)
    lane = lax.broadcasted_iota(jnp.int32, (blk, LANES), 1)
    first = lane < HEAD_DIM
    q = q_ref[...].astype(F32)
    q_heads = (jnp.where(first, q, 0.0).astype(BF16), jnp.where(first, 0.0, q).astype(BF16))
    den_lane = (HEAD_DIM, 0)
    den_cols = [jnp.where(lane == dl, 1.0, 0.0) for dl in den_lane]
    row = lax.broadcasted_iota(jnp.int32, (blk, blk), 0)
    col = lax.broadcasted_iota(jnp.int32, (blk, blk), 1)
    causal = col <= row
    acc_ref[...] = jnp.zeros_like(acc_ref)
    m_ref[...] = jnp.full_like(m_ref, NEG_BIG)

    def sweep(j, diagonal):
        start = pl.multiple_of(j * blk, blk)
        k = k_ref[pl.ds(start, blk), :]
        v = v_ref[pl.ds(start, blk), :].astype(F32)
        v_heads = (jnp.where(first, v, den_cols[0]).astype(BF16),
                   jnp.where(first, den_cols[1], v).astype(BF16))
        for hd in range(2):
            s = _dot_nt(q_heads[hd], k) + nf_ref[hd:hd + 1, pl.ds(start, blk)]
            if diagonal:
                s = jnp.where(causal, s, NEG_BIG)
            m_prev = m_ref[hd]
            m_new = jnp.maximum(m_prev, jnp.broadcast_to(jnp.max(s, axis=1, keepdims=True), m_prev.shape))
            p = jnp.exp(s - jnp.concatenate([m_new] * (blk // LANES), axis=1))
            acc_ref[hd] = jnp.exp(m_prev - m_new) * acc_ref[hd] + _dot(p.astype(BF16), v_heads[hd])
            m_ref[hd] = m_new

    sweep(i, True)
    lax.fori_loop(0, i, lambda j, c: (sweep(i - 1 - j, False), c)[1], 0)

    acc_a = acc_ref[0]
    acc_b = acc_ref[1]
    den_a = jnp.broadcast_to(acc_a[:, den_lane[0]:den_lane[0] + 1], acc_a.shape)
    den_b = jnp.broadcast_to(acc_b[:, den_lane[1]:den_lane[1] + 1], acc_b.shape)
    o_ref[...] = jnp.where(first, acc_a / den_a, acc_b / den_b).astype(o_ref.dtype)


def _forgetting_attention(q, k, v, neg_f, batch, seq):
    m, width = q.shape
    nq = seq // ATT_BLOCK
    n_pairs = width // LANES
    neg_f = neg_f.reshape(n_pairs, 2, m)
    qspec = pl.BlockSpec((ATT_BLOCK, LANES), lambda b, p, i: (b * nq + i, p))
    kvspec = pl.BlockSpec((seq, LANES), lambda b, p, i: (b, p))
    return pl.pallas_call(
        _forget_kernel,
        grid=(batch, n_pairs, nq),
        in_specs=[qspec, kvspec, kvspec,
                  pl.BlockSpec((None, 2, seq), lambda b, p, i: (p, 0, b))],
        out_specs=qspec,
        out_shape=jax.ShapeDtypeStruct((m, width), BF16),
        scratch_shapes=[pltpu.VMEM((2, ATT_BLOCK, LANES), F32),
                        pltpu.VMEM((2, ATT_BLOCK, LANES), F32)],
        compiler_params=_params("parallel", "parallel", "arbitrary"),
        name="forgetting_attention",
    )(q, k, v, neg_f)


def _pool_kernel(x_ref, w_ref, s_ref, o_ref, pad_ref, *, chunk):
    g = pl.program_id(1)
    seq = x_ref.shape[0]
    pad_ref[0:MAX_WINDOW, :] = jnp.zeros((MAX_WINDOW, LANES), F32)
    pad_ref[MAX_WINDOW:, :] = x_ref[...]
    w = w_ref[...].astype(BF16)
    for gi, win in enumerate(POOL_WINDOWS):
        @pl.when(g == gi)
        def _(win=win):
            for r0 in range(0, seq, chunk):
                x = pad_ref[MAX_WINDOW + r0:MAX_WINDOW + r0 + chunk, :]
                total = x
                for back in range(1, win):
                    total = total + pad_ref[MAX_WINDOW + r0 - back:MAX_WINDOW + r0 - back + chunk, :]
                pos = r0 + lax.broadcasted_iota(jnp.int32, (chunk, LANES), 0)
                count = jnp.minimum(pos + 1, win).astype(F32)
                pooled = total / count - x
                o_ref[r0:r0 + chunk, :] = (_dot(pooled.astype(BF16), w) * s_ref[...]).astype(o_ref.dtype)


def _multiscale_pool(xd, pool_w, pool_scale, batch, seq):
    m, width = xd.shape
    n_groups = width // LANES
    spec = pl.BlockSpec((seq, LANES), lambda b, g: (b, g))
    return pl.pallas_call(
        functools.partial(_pool_kernel, chunk=min(seq, 512)),
        grid=(batch, n_groups),
        in_specs=[spec,
                  pl.BlockSpec((None, LANES, LANES), lambda b, g: (g, 0, 0)),
                  pl.BlockSpec((1, LANES), lambda b, g: (0, g))],
        out_specs=spec,
        out_shape=jax.ShapeDtypeStruct((m, width), BF16),
        scratch_shapes=[pltpu.VMEM((seq + MAX_WINDOW, LANES), F32)],
        compiler_params=_params("parallel", "parallel"),
        name="multiscale_pool",
    )(xd, pool_w, pool_scale)


def _out_proj_kernel(h_ref, ya_ref, yb_ref, w_ref, o_ref):
    half = ya_ref.shape[1]
    o_ref[...] = (h_ref[...] + _dot(ya_ref[...], w_ref[0:half, :])
                  + _dot(yb_ref[...], w_ref[half:, :]))


def _out_proj(h, ya, yb, w):
    m, d = h.shape
    row = pl.BlockSpec((ROW_TILE, d), lambda i: (i, 0))
    half = pl.BlockSpec((ROW_TILE, ya.shape[1]), lambda i: (i, 0))
    return pl.pallas_call(
        _out_proj_kernel,
        grid=(m // ROW_TILE,),
        in_specs=[row, half, half, pl.BlockSpec(w.shape, lambda i: (0, 0))],
        out_specs=row,
        out_shape=jax.ShapeDtypeStruct((m, d), F32),
        compiler_params=_params("parallel"),
        name="out_proj",
    )(h, ya, yb, w)


def _route(logits):
    lane = lax.broadcasted_iota(jnp.int32, logits.shape, 1)
    is_group = lane < N_GROUPS
    gl = jnp.where(is_group, logits, NEG_BIG)
    g_max = jnp.max(gl, axis=1, keepdims=True)
    g_idx = jnp.min(jnp.where(gl == g_max, lane, LANES), axis=1, keepdims=True)
    g_den = jnp.sum(jnp.where(is_group, jnp.exp(gl - g_max), 0.0), axis=1, keepdims=True)
    g_p = 1.0 / g_den
    lo = N_GROUPS + g_idx * EXPERTS_PER_GROUP
    in_group = jnp.logical_and(lane >= lo, lane < lo + EXPERTS_PER_GROUP)
    el = jnp.where(in_group, logits, NEG_BIG)
    top1 = jnp.max(el, axis=1, keepdims=True)
    idx1 = jnp.min(jnp.where(el == top1, lane, LANES), axis=1, keepdims=True)
    el2 = jnp.where(lane == idx1, NEG_BIG, el)
    top2 = jnp.max(el2, axis=1, keepdims=True)
    idx2 = jnp.min(jnp.where(el2 == top2, lane, LANES), axis=1, keepdims=True)
    e2 = jnp.exp(top2 - top1)
    den = 1.0 + e2
    w1 = (1.0 / den) * g_p
    w2 = (e2 / den) * g_p
    return jnp.where(lane == idx1, w1, 0.0) + jnp.where(lane == idx2, w2, 0.0)


def _moe_kernel(h_ref, g_ref, wr_hi_ref, wr_lo_ref, br_ref, wgu_ref, wd_ref, o_ref,
                hn_ref, gate_ref, acc_ref):
    e = pl.program_id(1)

    @pl.when(e == 0)
    def _():
        hn = _rms_normalize(h_ref[...]) * g_ref[...]
        hn_ref[...] = hn.astype(BF16)
        logits = _dot3(hn, wr_hi_ref[...], wr_lo_ref[...]) + br_ref[...]
        gate_ref[...] = _route(logits)
        acc_ref[...] = jnp.zeros_like(acc_ref)

    lane = lax.broadcasted_iota(jnp.int32, gate_ref.shape, 1)
    gate = jnp.sum(jnp.where(lane == e + N_GROUPS, gate_ref[...], 0.0), axis=1, keepdims=True)
    gu = _dot(hn_ref[...], wgu_ref[...])
    hg = gu[:, :D_EXPERT]
    hu = gu[:, D_EXPERT:]
    act = (hg * jax.nn.sigmoid(hg)) * hu * gate
    acc_ref[...] += _dot(act.astype(BF16), wd_ref[...])

    @pl.when(e == N_EXPERTS - 1)
    def _():
        o_ref[...] = h_ref[...] + acc_ref[...]


def _moe(h, g, wr_hi, wr_lo, br, wgu, wd):
    m, d = h.shape
    row = pl.BlockSpec((MOE_TILE, d), lambda i, e: (i, 0))
    full = lambda a: pl.BlockSpec(a.shape, lambda i, e: (0,) * a.ndim)
    return pl.pallas_call(
        _moe_kernel,
        grid=(m // MOE_TILE, N_EXPERTS),
        in_specs=[row, full(g), full(wr_hi), full(wr_lo), full(br),
                  pl.BlockSpec((None, d, 2 * D_EXPERT), lambda i, e: (e, 0, 0)),
                  pl.BlockSpec((None, D_EXPERT, d), lambda i, e: (e, 0, 0))],
        out_specs=row,
        out_shape=jax.ShapeDtypeStruct((m, d), F32),
        scratch_shapes=[pltpu.VMEM((MOE_TILE, d), BF16),
                        pltpu.VMEM((MOE_TILE, LANES), F32),
                        pltpu.VMEM((MOE_TILE, d), F32)],
        compiler_params=_params("parallel", "arbitrary"),
        name="hierarchical_moe",
    )(h, g, wr_hi, wr_lo, br, wgu, wd)


def _ple_kernel(h_ref, p_ref, wp_ref, gn_ref, wg_ref, o_ref):
    h = h_ref[...]
    e = _rms_normalize(_dot(p_ref[...].astype(BF16), wp_ref[...])) * gn_ref[...]
    gate = jax.nn.sigmoid(_dot(_rms_normalize(h).astype(BF16), wg_ref[...]))
    o_ref[...] = h + gate * e


def _per_layer_embedding(h, p, wp, gn, wg):
    m, d = h.shape
    row = pl.BlockSpec((ROW_TILE, d), lambda i: (i, 0))
    full = lambda a: pl.BlockSpec(a.shape, lambda i: (0,) * a.ndim)
    return pl.pallas_call(
        _ple_kernel,
        grid=(m // ROW_TILE,),
        in_specs=[row, pl.BlockSpec((ROW_TILE, p.shape[1]), lambda i: (i, 0)),
                  full(wp), full(gn), full(wg)],
        out_specs=row,
        out_shape=jax.ShapeDtypeStruct((m, d), F32),
        compiler_params=_params("parallel"),
        name="per_layer_embedding",
    )(h, p, wp, gn, wg)


def _row(a):
    return a.reshape(1, -1).astype(F32)


def _hi_lo(w):
    hi = w.astype(BF16)
    return hi, (w - hi.astype(F32)).astype(BF16)


def _even_mixer(h, g, w_in, sgu_w, sgu_b, sgu_norm, w_out, batch, seq):
    u, va, q, k, vb = _even_in_proj(h, _row(g), w_in.astype(BF16))
    bias = jnp.repeat(sgu_b.T, HEAD_DIM, axis=1)
    ya = _spatial_gating(u, va, sgu_w, bias, _row(sgu_norm))
    yb = _stick_breaking(q, k, vb, batch, seq)
    return _out_proj(h, ya, yb, w_out.astype(BF16))


def _odd_mixer(h, g, w_in, forget_bias, q_norm, k_norm, pool_w, pool_scale, w_out, batch, seq):
    n_heads = forget_bias.shape[0]
    width = n_heads * HEAD_DIM
    w_main = jnp.concatenate([w_in[:, :3 * width], w_in[:, 3 * width + n_heads:]], axis=1).astype(BF16)
    wf_hi, wf_lo = _hi_lo(w_in[:, 3 * width:3 * width + n_heads].T)
    gain = lambda t: jnp.tile(t, LANES // HEAD_DIM).reshape(1, LANES).astype(F32)
    q, k, v, xd, neg_f = _odd_in_proj(h, _row(g), w_main, wf_hi, wf_lo,
                                      forget_bias.reshape(n_heads, 1).astype(F32),
                                      gain(q_norm), gain(k_norm), seq)
    yc = _forgetting_attention(q, k, v, neg_f, batch, seq)
    yd = _multiscale_pool(xd, pool_w, _row(pool_scale), batch, seq)
    return _out_proj(h, yc, yd, w_out.astype(BF16))


def _moe_layer(h, g, rg_w, rg_b, re_w, re_b, w_gate, w_up, w_down):
    d = h.shape[1]
    n_route = N_GROUPS + N_EXPERTS
    wr = jnp.zeros((d, LANES), F32).at[:, :N_GROUPS].set(rg_w).at[:, N_GROUPS:n_route].set(re_w)
    br = jnp.zeros((1, LANES), F32).at[0, :N_GROUPS].set(rg_b).at[0, N_GROUPS:n_route].set(re_b)
    wr_hi, wr_lo = _hi_lo(wr)
    wgu = jnp.concatenate([w_gate, w_up], axis=2).astype(BF16)
    return _moe(h, _row(g), wr_hi, wr_lo, br, wgu, w_down.astype(BF16))


def kernel(x, p, norm_mix, norm_ffn, even_w_in, sgu_w, sgu_b, sgu_norm, even_w_out, odd_w_in, forget_bias, q_norm, k_norm, pool_w, pool_scale, odd_w_out, router_group_w, router_group_b, router_expert_w, router_expert_b, expert_w_gate, expert_w_up, expert_w_down, ple_w_proj, ple_norm, ple_w_gate):
    batch, seq, d = x.shape
    depth = p.shape[0]
    h = x.reshape(batch * seq, d)
    for i in range(depth):
        j = i // 2
        if i % 2 == 0:
            h = _even_mixer(h, norm_mix[i], even_w_in[j], sgu_w[j], sgu_b[j], sgu_norm[j],
                            even_w_out[j], batch, seq)
        else:
            h = _odd_mixer(h, norm_mix[i], odd_w_in[j], forget_bias[j], q_norm[j], k_norm[j],
                           pool_w[j], pool_scale[j], odd_w_out[j], batch, seq)
        h = _moe_layer(h, norm_ffn[i], router_group_w[i], router_group_b[i], router_expert_w[i],
                       router_expert_b[i], expert_w_gate[i], expert_w_up[i], expert_w_down[i])
        h = _per_layer_embedding(h, p[i].reshape(batch * seq, -1), ple_w_proj[i].astype(BF16),
                                 _row(ple_norm[i]), ple_w_gate[i].astype(BF16))
    return h.reshape(batch, seq, d)
```

```python
import functools

import jax
import jax.numpy as jnp
from jax import lax
from jax.experimental import pallas as pl
from jax.experimental.pallas import tpu as pltpu

F32 = jnp.float32
BF16 = jnp.bfloat16

HEAD_DIM = 64
LANES = 128
CHUNK = 128
POOL_WINDOWS = (2, 4, 8, 16)
MAX_WINDOW = max(POOL_WINDOWS)
N_GROUPS = 4
EXPERTS_PER_GROUP = 4
N_EXPERTS = N_GROUPS * EXPERTS_PER_GROUP
D_EXPERT = 256
RMS_EPS = 1e-6
ATT_BLOCK = 256
FORGET_Q_BLOCK = 512
FORGET_K_BLOCK = 512
LOG2_E = 1.4426950408889634
ROW_TILE = 512
MOE_TILE = 1024
VMEM_LIMIT = 48 * 1024 * 1024
EXP2_ZERO_BELOW = -150.0
NEG_BIG = -1e30


def _params(*sem):
    return pltpu.CompilerParams(dimension_semantics=sem, vmem_limit_bytes=VMEM_LIMIT)


def _split_bf16(x):
    hi = x.astype(BF16)
    lo = (x - hi.astype(F32)).astype(BF16)
    return hi, lo


def _dot(a, b):
    return jnp.dot(a, b, preferred_element_type=F32)


def _dot_nt(a, b):
    return lax.dot_general(a, b, (((1,), (1,)), ((), ())), preferred_element_type=F32)


def _dot_exact_rhs(x, m):
    hi, lo = _split_bf16(x)
    return _dot(hi, m) + _dot(lo, m)


def _dot3(x, w_hi, w_lo):
    hi, lo = _split_bf16(x)
    return _dot(hi, w_hi) + (_dot(hi, w_lo) + _dot(lo, w_hi))


def _dot3_nt(w_hi, w_lo, x):
    hi, lo = _split_bf16(x)
    return _dot_nt(w_hi, hi) + (_dot_nt(w_lo, hi) + _dot_nt(w_hi, lo))


def _rms_normalize(x):
    return x * lax.rsqrt(jnp.mean(x * x, axis=-1, keepdims=True) + RMS_EPS)


def _log_sigmoid(x):
    return jnp.minimum(x, 0.0) - jnp.log1p(jnp.exp(-jnp.abs(x)))


def _gelu_tanh(x):
    return 0.5 * x * (1.0 + jnp.tanh(0.7978845608028654 * (x + 0.044715 * (x * x * x))))


def _head_group_matrix():
    r = lax.broadcasted_iota(jnp.int32, (LANES, LANES), 0) // HEAD_DIM
    c = lax.broadcasted_iota(jnp.int32, (LANES, LANES), 1) // HEAD_DIM
    return jnp.where(r == c, 1.0, 0.0).astype(BF16)


def _first_head_mask(shape):
    return lax.broadcasted_iota(jnp.int32, shape, len(shape) - 1) < HEAD_DIM


def _even_in_kernel(h_ref, g_ref, w_ref, u_ref, va_ref, q_ref, k_ref, vb_ref, *, width):
    hn = (_rms_normalize(h_ref[...]) * g_ref[...]).astype(BF16)
    outs = (u_ref, va_ref, q_ref, k_ref, vb_ref)
    for c, o in enumerate(outs):
        r = _dot(hn, w_ref[:, c * width:(c + 1) * width])
        if o is q_ref:
            r = r * (HEAD_DIM ** -0.5 * LOG2_E)
        o[...] = r.astype(o.dtype)


def _even_in_proj(h, g, w):
    m, d = h.shape
    width = w.shape[1] // 5
    out = jax.ShapeDtypeStruct((m, width), BF16)
    row = pl.BlockSpec((ROW_TILE, width), lambda i: (i, 0))
    return pl.pallas_call(
        functools.partial(_even_in_kernel, width=width),
        grid=(m // ROW_TILE,),
        in_specs=[pl.BlockSpec((ROW_TILE, d), lambda i: (i, 0)),
                  pl.BlockSpec((1, d), lambda i: (0, 0)),
                  pl.BlockSpec(w.shape, lambda i: (0, 0))],
        out_specs=[row] * 5,
        out_shape=[out] * 5,
        compiler_params=_params("parallel"),
        name="even_in_proj",
    )(h, g, w)


def _sgu_kernel(u_ref, v_ref, w_ref, b_ref, g_ref, o_ref):
    n_pairs = u_ref.shape[1] // LANES
    n_chunks = u_ref.shape[0] // CHUNK
    gmat = _head_group_matrix()
    first = _first_head_mask((CHUNK, LANES))
    tril = (lax.broadcasted_iota(jnp.int32, (CHUNK, CHUNK), 0)
            >= lax.broadcasted_iota(jnp.int32, (CHUNK, CHUNK), 1))
    for p in range(n_pairs):
        cols = slice(p * LANES, (p + 1) * LANES)
        w_a = jnp.where(tril, w_ref[2 * p], 0.0).astype(BF16)
        w_b = jnp.where(tril, w_ref[2 * p + 1], 0.0).astype(BF16)
        v = _gelu_tanh(v_ref[:, cols].astype(F32))
        ms = _dot_exact_rhs(v * v, gmat) * (1.0 / HEAD_DIM)
        vn = v * lax.rsqrt(ms + RMS_EPS) * g_ref[:, cols]
        for c in range(n_chunks):
            rows = slice(c * CHUNK, (c + 1) * CHUNK)
            vc = vn[rows]
            mixed = (_dot(w_a, jnp.where(first, vc, 0.0).astype(BF16))
                     + _dot(w_b, jnp.where(first, 0.0, vc).astype(BF16)) + b_ref[:, cols])
            u = _gelu_tanh(u_ref[rows, cols].astype(F32))
            o_ref[rows, cols] = (u * mixed).astype(o_ref.dtype)


def _spatial_gating(u, va, w_s, bias, g_v):
    m, width = u.shape
    row = pl.BlockSpec((ROW_TILE, width), lambda i: (i, 0))
    return pl.pallas_call(
        _sgu_kernel,
        grid=(m // ROW_TILE,),
        in_specs=[row, row,
                  pl.BlockSpec(w_s.shape, lambda i: (0, 0, 0)),
                  pl.BlockSpec(bias.shape, lambda i: (0, 0)),
                  pl.BlockSpec(g_v.shape, lambda i: (0, 0))],
        out_specs=row,
        out_shape=jax.ShapeDtypeStruct((m, width), BF16),
        compiler_params=_params("parallel"),
        name="spatial_gating",
    )(u, va, w_s, bias, g_v)


def _stick_kernel(q_ref, k_ref, v_ref, o_ref, acc_ref, r_ref):
    blk = ATT_BLOCK
    n_pairs = q_ref.shape[1] // LANES
    i = pl.program_id(1)
    first_q = _first_head_mask((blk, LANES))
    row = lax.broadcasted_iota(jnp.int32, (blk, blk), 0)
    col = lax.broadcasted_iota(jnp.int32, (blk, blk), 1)
    suffix = jnp.where(row >= col, 1.0, 0.0).astype(BF16)
    suffix2 = jnp.concatenate([suffix, suffix], axis=0)
    strict = col < row
    acc_ref[...] = jnp.zeros_like(acc_ref)
    r_ref[...] = jnp.zeros_like(r_ref)

    def sweep(j, diagonal):
        start = pl.multiple_of(j * blk, blk)
        n_heads = 2 * n_pairs
        zs = []
        for p in range(n_pairs):
            cols = slice(p * LANES, (p + 1) * LANES)
            q = q_ref[:, cols].astype(F32)
            q2 = jnp.concatenate([jnp.where(first_q, q, 0.0), jnp.where(first_q, 0.0, q)], axis=0)
            zs.append(_dot_nt(q2.astype(BF16), k_ref[pl.ds(start, blk), cols]))
        z = jnp.concatenate(zs, axis=0)
        nz = -z
        log_1m_beta = jnp.minimum(nz, 0.0) - jnp.log2(1.0 + jnp.exp2(jnp.minimum(z, nz)))
        if diagonal:
            strict_all = jnp.concatenate([strict] * n_heads, axis=0)
            log_1m_beta = jnp.where(strict_all, log_1m_beta, 0.0)
        hi, lo = _split_bf16(log_1m_beta)
        tail = _dot(jnp.concatenate([hi, lo], axis=1), suffix2)
        r = r_ref[...]
        weights = jnp.exp2(z + tail + jnp.concatenate([r] * (blk // LANES), axis=1))
        if diagonal:
            weights = jnp.where(strict_all, weights, 0.0)
        weights = weights.astype(BF16)
        r_ref[...] = r + jnp.broadcast_to(tail[:, 0:1], r.shape)
        for p in range(n_pairs):
            cols = slice(p * LANES, (p + 1) * LANES)
            v = v_ref[pl.ds(start, blk), cols].astype(F32)
            v2 = jnp.concatenate([jnp.where(first_q, v, 0.0), jnp.where(first_q, 0.0, v)], axis=0)
            w2 = jnp.concatenate([weights[2 * p * blk:(2 * p + 1) * blk],
                                  weights[(2 * p + 1) * blk:(2 * p + 2) * blk]], axis=1)
            acc_ref[p] += _dot(w2, v2.astype(BF16))

    sweep(i, True)

    def cond(carry):
        j, live = carry
        return jnp.logical_and(j >= 0, live)

    def any_live():
        return jnp.max(r_ref[...]) > EXP2_ZERO_BELOW

    def body(carry):
        j, _ = carry
        sweep(j, False)
        return j - 1, any_live()

    lax.while_loop(cond, body, (i - 1, any_live()))
    for p in range(n_pairs):
        o_ref[:, p * LANES:(p + 1) * LANES] = acc_ref[p].astype(o_ref.dtype)


def _stick_breaking(q, k, v, batch, seq):
    m, width = q.shape
    nq = seq // ATT_BLOCK
    n_pairs = width // LANES
    qspec = pl.BlockSpec((ATT_BLOCK, width), lambda b, i: (b * nq + i, 0))
    kvspec = pl.BlockSpec((seq, width), lambda b, i: (b, 0))
    return pl.pallas_call(
        _stick_kernel,
        grid=(batch, nq),
        in_specs=[qspec, kvspec, kvspec],
        out_specs=qspec,
        out_shape=jax.ShapeDtypeStruct((m, width), BF16),
        scratch_shapes=[pltpu.VMEM((n_pairs, ATT_BLOCK, LANES), F32),
                        pltpu.VMEM((2 * n_pairs * ATT_BLOCK, LANES), F32)],
        compiler_params=_params("parallel", "arbitrary"),
        name="stick_breaking",
    )(q, k, v)


def _odd_in_kernel(h_ref, g_ref, w_ref, wf_hi_ref, wf_lo_ref, fb_ref, qn_ref, kn_ref,
                   q_ref, k_ref, v_ref, xd_ref, nf_ref, carry_ref, *, width, tiles_per_seq):
    i = pl.program_id(0)
    hn32 = _rms_normalize(h_ref[...]) * g_ref[...]
    hn = hn32.astype(BF16)
    gmat = _head_group_matrix()

    def head_norm(x, gain_ref):
        parts = []
        for p in range(width // LANES):
            xp = x[:, p * LANES:(p + 1) * LANES]
            ms = _dot_exact_rhs(xp * xp, gmat) * (1.0 / HEAD_DIM)
            parts.append(xp * lax.rsqrt(ms + RMS_EPS) * gain_ref[...])
        return jnp.concatenate(parts, axis=1)

    qf = head_norm(_dot(hn, w_ref[:, 0:width]), qn_ref) * (HEAD_DIM ** -0.5 * LOG2_E)
    q_ref[...] = qf.astype(q_ref.dtype)
    k_ref[...] = head_norm(_dot(hn, w_ref[:, width:2 * width]), kn_ref).astype(k_ref.dtype)
    v_ref[...] = _dot(hn, w_ref[:, 2 * width:3 * width]).astype(v_ref.dtype)
    xd_ref[...] = _dot(hn, w_ref[:, 3 * width:4 * width]).astype(xd_ref.dtype)

    f_logit = _dot3_nt(wf_hi_ref[...], wf_lo_ref[...], hn32)
    log_f = _log_sigmoid(f_logit + fb_ref[...])
    n_heads, tile = log_f.shape

    @pl.when(i % tiles_per_seq == 0)
    def _():
        carry_ref[...] = jnp.zeros_like(carry_ref)

    r = lax.broadcasted_iota(jnp.int32, (LANES, LANES), 0)
    c = lax.broadcasted_iota(jnp.int32, (LANES, LANES), 1)
    prefix = jnp.where(r <= c, 1.0, 0.0).astype(BF16)
    carry = carry_ref[...]
    for s in range(tile // LANES):
        x = log_f[:, s * LANES:(s + 1) * LANES]
        x1 = x.astype(BF16)
        rem = x - x1.astype(F32)
        x2 = rem.astype(BF16)
        x3 = (rem - x2.astype(F32)).astype(BF16)
        cum = _dot(x1, prefix) + (_dot(x2, prefix) + _dot(x3, prefix)) + carry
        nf_ref[:, s * LANES:(s + 1) * LANES] = cum * (-LOG2_E)
        carry = jnp.broadcast_to(cum[:, LANES - 1:LANES], carry.shape)
    carry_ref[...] = carry


def _odd_in_proj(h, g, w, wf_hi, wf_lo, fb, qn, kn, seq):
    m, d = h.shape
    width = w.shape[1] // 4
    n_heads = wf_hi.shape[0]
    row = lambda dt: pl.BlockSpec((ROW_TILE, width), lambda i: (i, 0))
    full = lambda a: pl.BlockSpec(a.shape, lambda i: (0,) * a.ndim)
    return pl.pallas_call(
        functools.partial(_odd_in_kernel, width=width, tiles_per_seq=seq // ROW_TILE),
        grid=(m // ROW_TILE,),
        in_specs=[pl.BlockSpec((ROW_TILE, d), lambda i: (i, 0)),
                  full(g), full(w), full(wf_hi), full(wf_lo), full(fb), full(qn), full(kn)],
        out_specs=[row(BF16), row(BF16), row(BF16), row(F32),
                   pl.BlockSpec((n_heads, ROW_TILE), lambda i: (0, i))],
        out_shape=[jax.ShapeDtypeStruct((m, width), BF16)] * 3
                  + [jax.ShapeDtypeStruct((m, width), F32),
                     jax.ShapeDtypeStruct((n_heads, m), F32)],
        scratch_shapes=[pltpu.VMEM((n_heads, LANES), F32)],
        compiler_params=_params("arbitrary"),
        name="odd_in_proj",
    )(h, g, w, wf_hi, wf_lo, fb, qn, kn)


def _forget_kernel(q_ref, k_ref, v_ref, nf_ref, o_ref, acc_ref, m_ref):
    tq, tk = FORGET_Q_BLOCK, FORGET_K_BLOCK
    i = pl.program_id(2)
    first_q = _first_head_mask((tq, LANES))
    lane = lax.broadcasted_iota(jnp.int32, (tk, LANES), 1)
    first_k = lane < HEAD_DIM
    q = q_ref[...].astype(F32)
    q_heads = (jnp.where(first_q, q, 0.0).astype(BF16), jnp.where(first_q, 0.0, q).astype(BF16))
    den_lane = (HEAD_DIM, 0)
    den_cols = [jnp.where(lane == dl, 1.0, 0.0) for dl in den_lane]
    acc_ref[...] = jnp.zeros_like(acc_ref)
    m_ref[...] = jnp.full_like(m_ref, NEG_BIG)
    diag_block = (i * tq) // tk

    def sweep(j, diagonal):
        start = pl.multiple_of(j * tk, tk)
        k = k_ref[pl.ds(start, tk), :]
        v = v_ref[pl.ds(start, tk), :].astype(F32)
        v_heads = (jnp.where(first_k, v, den_cols[0]).astype(BF16),
                   jnp.where(first_k, den_cols[1], v).astype(BF16))
        if diagonal:
            q_pos = i * tq + lax.broadcasted_iota(jnp.int32, (tq, tk), 0)
            k_pos = start + lax.broadcasted_iota(jnp.int32, (tq, tk), 1)
            causal = k_pos <= q_pos
        m_prevs = [m_ref[hd] for hd in range(2)]
        accs = [acc_ref[hd] for hd in range(2)]
        m_news = []
        for hd in range(2):
            s = _dot_nt(q_heads[hd], k) + nf_ref[hd:hd + 1, pl.ds(start, tk)]
            if diagonal:
                s = jnp.where(causal, s, NEG_BIG)
            m_prev = m_prevs[hd]
            m_new = jnp.maximum(m_prev, jnp.broadcast_to(jnp.max(s, axis=1, keepdims=True), m_prev.shape))
            p = jnp.exp2(s - jnp.concatenate([m_new] * (tk // LANES), axis=1))
            accs[hd] = jnp.exp2(m_prev - m_new) * accs[hd] + _dot(p.astype(BF16), v_heads[hd])
            m_news.append(m_new)
        for hd in range(2):
            acc_ref[hd] = accs[hd]
            m_ref[hd] = m_news[hd]

    sweep(diag_block, True)
    lax.fori_loop(0, diag_block, lambda j, c: (sweep(diag_block - 1 - j, False), c)[1], 0)

    acc_a = acc_ref[0]
    acc_b = acc_ref[1]
    den_a = jnp.broadcast_to(acc_a[:, den_lane[0]:den_lane[0] + 1], acc_a.shape)
    den_b = jnp.broadcast_to(acc_b[:, den_lane[1]:den_lane[1] + 1], acc_b.shape)
    o_ref[...] = jnp.where(first_q, acc_a / den_a, acc_b / den_b).astype(o_ref.dtype)


def _forgetting_attention(q, k, v, neg_f, batch, seq):
    m, width = q.shape
    tq = FORGET_Q_BLOCK
    assert FORGET_K_BLOCK % tq == 0 and seq % FORGET_K_BLOCK == 0
    nq = seq // tq
    n_pairs = width // LANES
    neg_f = neg_f.reshape(n_pairs, 2, m)
    qspec = pl.BlockSpec((tq, LANES), lambda b, p, i: (b * nq + i, p))
    kvspec = pl.BlockSpec((seq, LANES), lambda b, p, i: (b, p))
    return pl.pallas_call(
        _forget_kernel,
        grid=(batch, n_pairs, nq),
        in_specs=[qspec, kvspec, kvspec,
                  pl.BlockSpec((None, 2, seq), lambda b, p, i: (p, 0, b))],
        out_specs=qspec,
        out_shape=jax.ShapeDtypeStruct((m, width), BF16),
        scratch_shapes=[pltpu.VMEM((2, tq, LANES), F32),
                        pltpu.VMEM((2, tq, LANES), F32)],
        compiler_params=_params("parallel", "parallel", "arbitrary"),
        name="forgetting_attention",
    )(q, k, v, neg_f)


def _pool_kernel(x_ref, w_ref, s_ref, o_ref, pad_ref, *, chunk):
    g = pl.program_id(1)
    seq = x_ref.shape[0]
    pad_ref[0:MAX_WINDOW, :] = jnp.zeros((MAX_WINDOW, LANES), F32)
    pad_ref[MAX_WINDOW:, :] = x_ref[...]
    w = w_ref[...].astype(BF16)
    for gi, win in enumerate(POOL_WINDOWS):
        @pl.when(g == gi)
        def _(win=win):
            for r0 in range(0, seq, chunk):
                x = pad_ref[MAX_WINDOW + r0:MAX_WINDOW + r0 + chunk, :]
                total = x
                for back in range(1, win):
                    total = total + pad_ref[MAX_WINDOW + r0 - back:MAX_WINDOW + r0 - back + chunk, :]
                pos = r0 + lax.broadcasted_iota(jnp.int32, (chunk, LANES), 0)
                count = jnp.minimum(pos + 1, win).astype(F32)
                pooled = total / count - x
                o_ref[r0:r0 + chunk, :] = (_dot(pooled.astype(BF16), w) * s_ref[...]).astype(o_ref.dtype)


def _multiscale_pool(xd, pool_w, pool_scale, batch, seq):
    m, width = xd.shape
    n_groups = width // LANES
    spec = pl.BlockSpec((seq, LANES), lambda b, g: (b, g))
    return pl.pallas_call(
        functools.partial(_pool_kernel, chunk=min(seq, 512)),
        grid=(batch, n_groups),
        in_specs=[spec,
                  pl.BlockSpec((None, LANES, LANES), lambda b, g: (g, 0, 0)),
                  pl.BlockSpec((1, LANES), lambda b, g: (0, g))],
        out_specs=spec,
        out_shape=jax.ShapeDtypeStruct((m, width), BF16),
        scratch_shapes=[pltpu.VMEM((seq + MAX_WINDOW, LANES), F32)],
        compiler_params=_params("parallel", "parallel"),
        name="multiscale_pool",
    )(xd, pool_w, pool_scale)


def _out_proj_kernel(h_ref, ya_ref, yb_ref, w_ref, o_ref):
    half = ya_ref.shape[1]
    o_ref[...] = (h_ref[...] + _dot(ya_ref[...], w_ref[0:half, :])
                  + _dot(yb_ref[...], w_ref[half:, :]))


def _out_proj(h, ya, yb, w):
    m, d = h.shape
    row = pl.BlockSpec((ROW_TILE, d), lambda i: (i, 0))
    half = pl.BlockSpec((ROW_TILE, ya.shape[1]), lambda i: (i, 0))
    return pl.pallas_call(
        _out_proj_kernel,
        grid=(m // ROW_TILE,),
        in_specs=[row, half, half, pl.BlockSpec(w.shape, lambda i: (0, 0))],
        out_specs=row,
        out_shape=jax.ShapeDtypeStruct((m, d), F32),
        compiler_params=_params("parallel"),
        name="out_proj",
    )(h, ya, yb, w)


def _route(logits):
    lane = lax.broadcasted_iota(jnp.int32, logits.shape, 1)
    is_group = lane < N_GROUPS
    gl = jnp.where(is_group, logits, NEG_BIG)
    g_max = jnp.max(gl, axis=1, keepdims=True)
    g_idx = jnp.min(jnp.where(gl == g_max, lane, LANES), axis=1, keepdims=True)
    g_den = jnp.sum(jnp.where(is_group, jnp.exp(gl - g_max), 0.0), axis=1, keepdims=True)
    g_p = 1.0 / g_den
    lo = N_GROUPS + g_idx * EXPERTS_PER_GROUP
    in_group = jnp.logical_and(lane >= lo, lane < lo + EXPERTS_PER_GROUP)
    el = jnp.where(in_group, logits, NEG_BIG)
    top1 = jnp.max(el, axis=1, keepdims=True)
    idx1 = jnp.min(jnp.where(el == top1, lane, LANES), axis=1, keepdims=True)
    el2 = jnp.where(lane == idx1, NEG_BIG, el)
    top2 = jnp.max(el2, axis=1, keepdims=True)
    idx2 = jnp.min(jnp.where(el2 == top2, lane, LANES), axis=1, keepdims=True)
    e2 = jnp.exp(top2 - top1)
    den = 1.0 + e2
    w1 = (1.0 / den) * g_p
    w2 = (e2 / den) * g_p
    return jnp.where(lane == idx1, w1, 0.0) + jnp.where(lane == idx2, w2, 0.0)


def _moe_kernel(h_ref, g_ref, wr_hi_ref, wr_lo_ref, br_ref, wgu_ref, wd_ref, o_ref,
                hn_ref, gate_ref, acc_ref):
    e = pl.program_id(1)

    @pl.when(e == 0)
    def _():
        hn = _rms_normalize(h_ref[...]) * g_ref[...]
        hn_ref[...] = hn.astype(BF16)
        logits = _dot3(hn, wr_hi_ref[...], wr_lo_ref[...]) + br_ref[...]
        gate_ref[...] = _route(logits)
        acc_ref[...] = jnp.zeros_like(acc_ref)

    lane = lax.broadcasted_iota(jnp.int32, gate_ref.shape, 1)
    gate = jnp.sum(jnp.where(lane == e + N_GROUPS, gate_ref[...], 0.0), axis=1, keepdims=True)
    gu = _dot(hn_ref[...], wgu_ref[...])
    hg = gu[:, :D_EXPERT]
    hu = gu[:, D_EXPERT:]
    act = (hg * jax.nn.sigmoid(hg)) * hu * gate
    acc_ref[...] += _dot(act.astype(BF16), wd_ref[...])

    @pl.when(e == N_EXPERTS - 1)
    def _():
        o_ref[...] = h_ref[...] + acc_ref[...]


def _moe(h, g, wr_hi, wr_lo, br, wgu, wd):
    m, d = h.shape
    row = pl.BlockSpec((MOE_TILE, d), lambda i, e: (i, 0))
    full = lambda a: pl.BlockSpec(a.shape, lambda i, e: (0,) * a.ndim)
    return pl.pallas_call(
        _moe_kernel,
        grid=(m // MOE_TILE, N_EXPERTS),
        in_specs=[row, full(g), full(wr_hi), full(wr_lo), full(br),
                  pl.BlockSpec((None, d, 2 * D_EXPERT), lambda i, e: (e, 0, 0)),
                  pl.BlockSpec((None, D_EXPERT, d), lambda i, e: (e, 0, 0))],
        out_specs=row,
        out_shape=jax.ShapeDtypeStruct((m, d), F32),
        scratch_shapes=[pltpu.VMEM((MOE_TILE, d), BF16),
                        pltpu.VMEM((MOE_TILE, LANES), F32),
                        pltpu.VMEM((MOE_TILE, d), F32)],
        compiler_params=_params("parallel", "arbitrary"),
        name="hierarchical_moe",
    )(h, g, wr_hi, wr_lo, br, wgu, wd)


def _ple_kernel(h_ref, p_ref, wp_ref, gn_ref, wg_ref, o_ref):
    h = h_ref[...]
    e = _rms_normalize(_dot(p_ref[...].astype(BF16), wp_ref[...])) * gn_ref[...]
    gate = jax.nn.sigmoid(_dot(_rms_normalize(h).astype(BF16), wg_ref[...]))
    o_ref[...] = h + gate * e


def _per_layer_embedding(h, p, wp, gn, wg):
    m, d = h.shape
    row = pl.BlockSpec((ROW_TILE, d), lambda i: (i, 0))
    full = lambda a: pl.BlockSpec(a.shape, lambda i: (0,) * a.ndim)
    return pl.pallas_call(
        _ple_kernel,
        grid=(m // ROW_TILE,),
        in_specs=[row, pl.BlockSpec((ROW_TILE, p.shape[1]), lambda i: (i, 0)),
                  full(wp), full(gn), full(wg)],
        out_specs=row,
        out_shape=jax.ShapeDtypeStruct((m, d), F32),
        compiler_params=_params("parallel"),
        name="per_layer_embedding",
    )(h, p, wp, gn, wg)


def _row(a):
    return a.reshape(1, -1).astype(F32)


def _hi_lo(w):
    hi = w.astype(BF16)
    return hi, (w - hi.astype(F32)).astype(BF16)


def _even_mixer(h, g, w_in, sgu_w, sgu_b, sgu_norm, w_out, batch, seq):
    u, va, q, k, vb = _even_in_proj(h, _row(g), w_in.astype(BF16))
    bias = jnp.repeat(sgu_b.T, HEAD_DIM, axis=1)
    ya = _spatial_gating(u, va, sgu_w, bias, _row(sgu_norm))
    yb = _stick_breaking(q, k, vb, batch, seq)
    return _out_proj(h, ya, yb, w_out.astype(BF16))


def _odd_mixer(h, g, w_in, forget_bias, q_norm, k_norm, pool_w, pool_scale, w_out, batch, seq):
    n_heads = forget_bias.shape[0]
    width = n_heads * HEAD_DIM
    w_main = jnp.concatenate([w_in[:, :3 * width], w_in[:, 3 * width + n_heads:]], axis=1).astype(BF16)
    wf_hi, wf_lo = _hi_lo(w_in[:, 3 * width:3 * width + n_heads].T)
    gain = lambda t: jnp.tile(t, LANES // HEAD_DIM).reshape(1, LANES).astype(F32)
    q, k, v, xd, neg_f = _odd_in_proj(h, _row(g), w_main, wf_hi, wf_lo,
                                      forget_bias.reshape(n_heads, 1).astype(F32),
                                      gain(q_norm), gain(k_norm), seq)
    yc = _forgetting_attention(q, k, v, neg_f, batch, seq)
    yd = _multiscale_pool(xd, pool_w, _row(pool_scale), batch, seq)
    return _out_proj(h, yc, yd, w_out.astype(BF16))


def _moe_layer(h, g, rg_w, rg_b, re_w, re_b, w_gate, w_up, w_down):
    d = h.shape[1]
    n_route = N_GROUPS + N_EXPERTS
    wr = jnp.zeros((d, LANES), F32).at[:, :N_GROUPS].set(rg_w).at[:, N_GROUPS:n_route].set(re_w)
    br = jnp.zeros((1, LANES), F32).at[0, :N_GROUPS].set(rg_b).at[0, N_GROUPS:n_route].set(re_b)
    wr_hi, wr_lo = _hi_lo(wr)
    wgu = jnp.concatenate([w_gate, w_up], axis=2).astype(BF16)
    return _moe(h, _row(g), wr_hi, wr_lo, br, wgu, w_down.astype(BF16))


def kernel(x, p, norm_mix, norm_ffn, even_w_in, sgu_w, sgu_b, sgu_norm, even_w_out, odd_w_in, forget_bias, q_norm, k_norm, pool_w, pool_scale, odd_w_out, router_group_w, router_group_b, router_expert_w, router_expert_b, expert_w_gate, expert_w_up, expert_w_down, ple_w_proj, ple_norm, ple_w_gate):
    batch, seq, d = x.shape
    depth = p.shape[0]
    h = x.reshape(batch * seq, d)
    for i in range(depth):
        j = i // 2
        if i % 2 == 0:
            h = _even_mixer(h, norm_mix[i], even_w_in[j], sgu_w[j], sgu_b[j], sgu_norm[j],
                            even_w_out[j], batch, seq)
        else:
            h = _odd_mixer(h, norm_mix[i], odd_w_in[j], forget_bias[j], q_norm[j], k_norm[j],
                           pool_w[j], pool_scale[j], odd_w_out[j], batch, seq)
        h = _moe_layer(h, norm_ffn[i], router_group_w[i], router_group_b[i], router_expert_w[i],
                       router_expert_b[i], expert_w_gate[i], expert_w_up[i], expert_w_down[i])
        h = _per_layer_embedding(h, p[i].reshape(batch * seq, -1), ple_w_proj[i].astype(BF16),
                                 _row(ple_norm[i]), ple_w_gate[i].astype(BF16))
    return h.reshape(batch, seq, d)
```

```python
import functools

import jax
import jax.numpy as jnp
from jax import lax
from jax.experimental import pallas as pl
from jax.experimental.pallas import tpu as pltpu

F32 = jnp.float32
BF16 = jnp.bfloat16

HEAD_DIM = 64
LANES = 128
CHUNK = 128
POOL_WINDOWS = (2, 4, 8, 16)
MAX_WINDOW = max(POOL_WINDOWS)
N_GROUPS = 4
EXPERTS_PER_GROUP = 4
N_EXPERTS = N_GROUPS * EXPERTS_PER_GROUP
D_EXPERT = 256
RMS_EPS = 1e-6
ATT_BLOCK = 256
FORGET_Q_BLOCK = 512
FORGET_K_BLOCK = 512
LOG2_E = 1.4426950408889634
ROW_TILE = 512
MOE_TILE = 512
MOE_WINDOW = 176
BF16_ROWS = 16
VMEM_LIMIT = 48 * 1024 * 1024
MOE_VMEM_LIMIT = 56 * 1024 * 1024
EXP2_ZERO_BELOW = -150.0
NEG_BIG = -1e30


def _params(*sem):
    return pltpu.CompilerParams(dimension_semantics=sem, vmem_limit_bytes=VMEM_LIMIT)


def _split_bf16(x):
    hi = x.astype(BF16)
    lo = (x - hi.astype(F32)).astype(BF16)
    return hi, lo


def _dot(a, b):
    return jnp.dot(a, b, preferred_element_type=F32)


def _dot_nt(a, b):
    return lax.dot_general(a, b, (((1,), (1,)), ((), ())), preferred_element_type=F32)


def _dot_exact_rhs(x, m):
    hi, lo = _split_bf16(x)
    return _dot(hi, m) + _dot(lo, m)


def _dot3(x, w_hi, w_lo):
    hi, lo = _split_bf16(x)
    return _dot(hi, w_hi) + (_dot(hi, w_lo) + _dot(lo, w_hi))


def _dot3_nt(w_hi, w_lo, x):
    hi, lo = _split_bf16(x)
    return _dot_nt(w_hi, hi) + (_dot_nt(w_lo, hi) + _dot_nt(w_hi, lo))


def _rms_normalize(x):
    return x * lax.rsqrt(jnp.mean(x * x, axis=-1, keepdims=True) + RMS_EPS)


def _log_sigmoid(x):
    return jnp.minimum(x, 0.0) - jnp.log1p(jnp.exp(-jnp.abs(x)))


def _gelu_tanh(x):
    return 0.5 * x * (1.0 + jnp.tanh(0.7978845608028654 * (x + 0.044715 * (x * x * x))))


def _head_group_matrix():
    r = lax.broadcasted_iota(jnp.int32, (LANES, LANES), 0) // HEAD_DIM
    c = lax.broadcasted_iota(jnp.int32, (LANES, LANES), 1) // HEAD_DIM
    return jnp.where(r == c, 1.0, 0.0).astype(BF16)


def _first_head_mask(shape):
    return lax.broadcasted_iota(jnp.int32, shape, len(shape) - 1) < HEAD_DIM


def _even_in_kernel(h_ref, g_ref, w_ref, u_ref, va_ref, q_ref, k_ref, vb_ref, *, width):
    hn = (_rms_normalize(h_ref[...]) * g_ref[...]).astype(BF16)
    outs = (u_ref, va_ref, q_ref, k_ref, vb_ref)
    for c, o in enumerate(outs):
        r = _dot(hn, w_ref[:, c * width:(c + 1) * width])
        if o is q_ref:
            r = r * (HEAD_DIM ** -0.5 * LOG2_E)
        o[...] = r.astype(o.dtype)


def _even_in_proj(h, g, w):
    m, d = h.shape
    width = w.shape[1] // 5
    out = jax.ShapeDtypeStruct((m, width), BF16)
    row = pl.BlockSpec((ROW_TILE, width), lambda i: (i, 0))
    return pl.pallas_call(
        functools.partial(_even_in_kernel, width=width),
        grid=(m // ROW_TILE,),
        in_specs=[pl.BlockSpec((ROW_TILE, d), lambda i: (i, 0)),
                  pl.BlockSpec((1, d), lambda i: (0, 0)),
                  pl.BlockSpec(w.shape, lambda i: (0, 0))],
        out_specs=[row] * 5,
        out_shape=[out] * 5,
        compiler_params=_params("parallel"),
        name="even_in_proj",
    )(h, g, w)


def _sgu_kernel(u_ref, v_ref, w_ref, b_ref, g_ref, o_ref):
    n_pairs = u_ref.shape[1] // LANES
    n_chunks = u_ref.shape[0] // CHUNK
    gmat = _head_group_matrix()
    first = _first_head_mask((CHUNK, LANES))
    tril = (lax.broadcasted_iota(jnp.int32, (CHUNK, CHUNK), 0)
            >= lax.broadcasted_iota(jnp.int32, (CHUNK, CHUNK), 1))
    for p in range(n_pairs):
        cols = slice(p * LANES, (p + 1) * LANES)
        w_a = jnp.where(tril, w_ref[2 * p], 0.0).astype(BF16)
        w_b = jnp.where(tril, w_ref[2 * p + 1], 0.0).astype(BF16)
        v = _gelu_tanh(v_ref[:, cols].astype(F32))
        ms = _dot_exact_rhs(v * v, gmat) * (1.0 / HEAD_DIM)
        vn = v * lax.rsqrt(ms + RMS_EPS) * g_ref[:, cols]
        for c in range(n_chunks):
            rows = slice(c * CHUNK, (c + 1) * CHUNK)
            vc = vn[rows]
            mixed = (_dot(w_a, jnp.where(first, vc, 0.0).astype(BF16))
                     + _dot(w_b, jnp.where(first, 0.0, vc).astype(BF16)) + b_ref[:, cols])
            u = _gelu_tanh(u_ref[rows, cols].astype(F32))
            o_ref[rows, cols] = (u * mixed).astype(o_ref.dtype)


def _spatial_gating(u, va, w_s, bias, g_v):
    m, width = u.shape
    row = pl.BlockSpec((ROW_TILE, width), lambda i: (i, 0))
    return pl.pallas_call(
        _sgu_kernel,
        grid=(m // ROW_TILE,),
        in_specs=[row, row,
                  pl.BlockSpec(w_s.shape, lambda i: (0, 0, 0)),
                  pl.BlockSpec(bias.shape, lambda i: (0, 0)),
                  pl.BlockSpec(g_v.shape, lambda i: (0, 0))],
        out_specs=row,
        out_shape=jax.ShapeDtypeStruct((m, width), BF16),
        compiler_params=_params("parallel"),
        name="spatial_gating",
    )(u, va, w_s, bias, g_v)


def _stick_kernel(q_ref, k_ref, v_ref, o_ref, acc_ref, r_ref):
    blk = ATT_BLOCK
    n_pairs = q_ref.shape[1] // LANES
    i = pl.program_id(1)
    first_q = _first_head_mask((blk, LANES))
    row = lax.broadcasted_iota(jnp.int32, (blk, blk), 0)
    col = lax.broadcasted_iota(jnp.int32, (blk, blk), 1)
    suffix = jnp.where(row >= col, 1.0, 0.0).astype(BF16)
    suffix2 = jnp.concatenate([suffix, suffix], axis=0)
    strict = col < row
    acc_ref[...] = jnp.zeros_like(acc_ref)
    r_ref[...] = jnp.zeros_like(r_ref)

    def sweep(j, diagonal):
        start = pl.multiple_of(j * blk, blk)
        n_heads = 2 * n_pairs
        zs = []
        for p in range(n_pairs):
            cols = slice(p * LANES, (p + 1) * LANES)
            q = q_ref[:, cols].astype(F32)
            q2 = jnp.concatenate([jnp.where(first_q, q, 0.0), jnp.where(first_q, 0.0, q)], axis=0)
            zs.append(_dot_nt(q2.astype(BF16), k_ref[pl.ds(start, blk), cols]))
        z = jnp.concatenate(zs, axis=0)
        nz = -z
        log_1m_beta = jnp.minimum(nz, 0.0) - jnp.log2(1.0 + jnp.exp2(jnp.minimum(z, nz)))
        if diagonal:
            strict_all = jnp.concatenate([strict] * n_heads, axis=0)
            log_1m_beta = jnp.where(strict_all, log_1m_beta, 0.0)
        hi, lo = _split_bf16(log_1m_beta)
        tail = _dot(jnp.concatenate([hi, lo], axis=1), suffix2)
        r = r_ref[...]
        weights = jnp.exp2(z + tail + jnp.concatenate([r] * (blk // LANES), axis=1))
        if diagonal:
            weights = jnp.where(strict_all, weights, 0.0)
        weights = weights.astype(BF16)
        r_ref[...] = r + jnp.broadcast_to(tail[:, 0:1], r.shape)
        for p in range(n_pairs):
            cols = slice(p * LANES, (p + 1) * LANES)
            v = v_ref[pl.ds(start, blk), cols].astype(F32)
            v2 = jnp.concatenate([jnp.where(first_q, v, 0.0), jnp.where(first_q, 0.0, v)], axis=0)
            w2 = jnp.concatenate([weights[2 * p * blk:(2 * p + 1) * blk],
                                  weights[(2 * p + 1) * blk:(2 * p + 2) * blk]], axis=1)
            acc_ref[p] += _dot(w2, v2.astype(BF16))

    sweep(i, True)

    def cond(carry):
        j, live = carry
        return jnp.logical_and(j >= 0, live)

    def any_live():
        return jnp.max(r_ref[...]) > EXP2_ZERO_BELOW

    def body(carry):
        j, _ = carry
        sweep(j, False)
        return j - 1, any_live()

    lax.while_loop(cond, body, (i - 1, any_live()))
    for p in range(n_pairs):
        o_ref[:, p * LANES:(p + 1) * LANES] = acc_ref[p].astype(o_ref.dtype)


def _stick_breaking(q, k, v, batch, seq):
    m, width = q.shape
    nq = seq // ATT_BLOCK
    n_pairs = width // LANES
    qspec = pl.BlockSpec((ATT_BLOCK, width), lambda b, i: (b * nq + i, 0))
    kvspec = pl.BlockSpec((seq, width), lambda b, i: (b, 0))
    return pl.pallas_call(
        _stick_kernel,
        grid=(batch, nq),
        in_specs=[qspec, kvspec, kvspec],
        out_specs=qspec,
        out_shape=jax.ShapeDtypeStruct((m, width), BF16),
        scratch_shapes=[pltpu.VMEM((n_pairs, ATT_BLOCK, LANES), F32),
                        pltpu.VMEM((2 * n_pairs * ATT_BLOCK, LANES), F32)],
        compiler_params=_params("parallel", "arbitrary"),
        name="stick_breaking",
    )(q, k, v)


def _odd_in_kernel(h_ref, g_ref, w_ref, wf_hi_ref, wf_lo_ref, fb_ref, qn_ref, kn_ref,
                   q_ref, k_ref, v_ref, xd_ref, nf_ref, carry_ref, *, width, tiles_per_seq):
    i = pl.program_id(0)
    hn32 = _rms_normalize(h_ref[...]) * g_ref[...]
    hn = hn32.astype(BF16)
    gmat = _head_group_matrix()

    def head_norm(x, gain_ref):
        parts = []
        for p in range(width // LANES):
            xp = x[:, p * LANES:(p + 1) * LANES]
            ms = _dot_exact_rhs(xp * xp, gmat) * (1.0 / HEAD_DIM)
            parts.append(xp * lax.rsqrt(ms + RMS_EPS) * gain_ref[...])
        return jnp.concatenate(parts, axis=1)

    qf = head_norm(_dot(hn, w_ref[:, 0:width]), qn_ref) * (HEAD_DIM ** -0.5 * LOG2_E)
    q_ref[...] = qf.astype(q_ref.dtype)
    k_ref[...] = head_norm(_dot(hn, w_ref[:, width:2 * width]), kn_ref).astype(k_ref.dtype)
    v_ref[...] = _dot(hn, w_ref[:, 2 * width:3 * width]).astype(v_ref.dtype)
    xd_ref[...] = _dot(hn, w_ref[:, 3 * width:4 * width]).astype(xd_ref.dtype)

    f_logit = _dot3_nt(wf_hi_ref[...], wf_lo_ref[...], hn32)
    log_f = _log_sigmoid(f_logit + fb_ref[...])
    n_heads, tile = log_f.shape

    @pl.when(i % tiles_per_seq == 0)
    def _():
        carry_ref[...] = jnp.zeros_like(carry_ref)

    r = lax.broadcasted_iota(jnp.int32, (LANES, LANES), 0)
    c = lax.broadcasted_iota(jnp.int32, (LANES, LANES), 1)
    prefix = jnp.where(r <= c, 1.0, 0.0).astype(BF16)
    carry = carry_ref[...]
    for s in range(tile // LANES):
        x = log_f[:, s * LANES:(s + 1) * LANES]
        x1 = x.astype(BF16)
        rem = x - x1.astype(F32)
        x2 = rem.astype(BF16)
        x3 = (rem - x2.astype(F32)).astype(BF16)
        cum = _dot(x1, prefix) + (_dot(x2, prefix) + _dot(x3, prefix)) + carry
        nf_ref[:, s * LANES:(s + 1) * LANES] = cum * (-LOG2_E)
        carry = jnp.broadcast_to(cum[:, LANES - 1:LANES], carry.shape)
    carry_ref[...] = carry


def _odd_in_proj(h, g, w, wf_hi, wf_lo, fb, qn, kn, seq):
    m, d = h.shape
    width = w.shape[1] // 4
    n_heads = wf_hi.shape[0]
    row = lambda dt: pl.BlockSpec((ROW_TILE, width), lambda i: (i, 0))
    full = lambda a: pl.BlockSpec(a.shape, lambda i: (0,) * a.ndim)
    return pl.pallas_call(
        functools.partial(_odd_in_kernel, width=width, tiles_per_seq=seq // ROW_TILE),
        grid=(m // ROW_TILE,),
        in_specs=[pl.BlockSpec((ROW_TILE, d), lambda i: (i, 0)),
                  full(g), full(w), full(wf_hi), full(wf_lo), full(fb), full(qn), full(kn)],
        out_specs=[row(BF16), row(BF16), row(BF16), row(F32),
                   pl.BlockSpec((n_heads, ROW_TILE), lambda i: (0, i))],
        out_shape=[jax.ShapeDtypeStruct((m, width), BF16)] * 3
                  + [jax.ShapeDtypeStruct((m, width), F32),
                     jax.ShapeDtypeStruct((n_heads, m), F32)],
        scratch_shapes=[pltpu.VMEM((n_heads, LANES), F32)],
        compiler_params=_params("arbitrary"),
        name="odd_in_proj",
    )(h, g, w, wf_hi, wf_lo, fb, qn, kn)


def _forget_kernel(qk_bound_ref, q_ref, k_ref, v_ref, nf_ref, o_ref, acc_ref, m_ref):
    tq, tk = FORGET_Q_BLOCK, FORGET_K_BLOCK
    i = pl.program_id(2)
    first_q = _first_head_mask((tq, LANES))
    lane = lax.broadcasted_iota(jnp.int32, (tk, LANES), 1)
    first_k = lane < HEAD_DIM
    q = q_ref[...].astype(F32)
    q_heads = (jnp.where(first_q, q, 0.0).astype(BF16), jnp.where(first_q, 0.0, q).astype(BF16))
    den_lane = (HEAD_DIM, 0)
    den_cols = [jnp.where(lane == dl, 1.0, 0.0) for dl in den_lane]
    acc_ref[...] = jnp.zeros_like(acc_ref)
    m_ref[...] = jnp.full_like(m_ref, NEG_BIG)
    diag_block = (i * tq) // tk

    def sweep(j, diagonal):
        start = pl.multiple_of(j * tk, tk)
        k = k_ref[pl.ds(start, tk), :]
        v = v_ref[pl.ds(start, tk), :].astype(F32)
        v_heads = (jnp.where(first_k, v, den_cols[0]).astype(BF16),
                   jnp.where(first_k, den_cols[1], v).astype(BF16))
        if diagonal:
            q_pos = i * tq + lax.broadcasted_iota(jnp.int32, (tq, tk), 0)
            k_pos = start + lax.broadcasted_iota(jnp.int32, (tq, tk), 1)
            causal = k_pos <= q_pos
        m_prevs = [m_ref[hd] for hd in range(2)]
        accs = [acc_ref[hd] for hd in range(2)]
        m_news = []
        for hd in range(2):
            s = _dot_nt(q_heads[hd], k) + nf_ref[hd:hd + 1, pl.ds(start, tk)]
            if diagonal:
                s = jnp.where(causal, s, NEG_BIG)
            m_prev = m_prevs[hd]
            m_new = jnp.maximum(m_prev, jnp.broadcast_to(jnp.max(s, axis=1, keepdims=True), m_prev.shape))
            p = jnp.exp2(s - jnp.concatenate([m_new] * (tk // LANES), axis=1))
            accs[hd] = jnp.exp2(m_prev - m_new) * accs[hd] + _dot(p.astype(BF16), v_heads[hd])
            m_news.append(m_new)
        for hd in range(2):
            acc_ref[hd] = accs[hd]
            m_ref[hd] = m_news[hd]

    def block_live(j):
        tail = pl.multiple_of(jnp.maximum(j, 0) * tk + tk - LANES, LANES)
        live = False
        for hd in range(2):
            top = qk_bound_ref[0, 0] + jnp.max(nf_ref[hd:hd + 1, pl.ds(tail, LANES)], axis=1, keepdims=True)
            live = jnp.logical_or(live, jnp.max(top - m_ref[hd]) > EXP2_ZERO_BELOW)
        return live

    sweep(diag_block, True)

    def cond(carry):
        j, live = carry
        return jnp.logical_and(j >= 0, live)

    def body(carry):
        j, _ = carry
        sweep(j, False)
        return j - 1, block_live(j - 1)

    lax.while_loop(cond, body, (diag_block - 1, block_live(diag_block - 1)))

    acc_a = acc_ref[0]
    acc_b = acc_ref[1]
    den_a = jnp.broadcast_to(acc_a[:, den_lane[0]:den_lane[0] + 1], acc_a.shape)
    den_b = jnp.broadcast_to(acc_b[:, den_lane[1]:den_lane[1] + 1], acc_b.shape)
    o_ref[...] = jnp.where(first_q, acc_a / den_a, acc_b / den_b).astype(o_ref.dtype)


def _forgetting_attention(qk_bound, q, k, v, neg_f, batch, seq):
    m, width = q.shape
    tq = FORGET_Q_BLOCK
    assert FORGET_K_BLOCK % tq == 0 and seq % FORGET_K_BLOCK == 0
    nq = seq // tq
    n_pairs = width // LANES
    neg_f = neg_f.reshape(n_pairs, 2, m)
    qspec = pl.BlockSpec((tq, LANES), lambda b, p, i: (b * nq + i, p))
    kvspec = pl.BlockSpec((seq, LANES), lambda b, p, i: (b, p))
    return pl.pallas_call(
        _forget_kernel,
        grid=(batch, n_pairs, nq),
        in_specs=[pl.BlockSpec(memory_space=pltpu.SMEM), qspec, kvspec, kvspec,
                  pl.BlockSpec((None, 2, seq), lambda b, p, i: (p, 0, b))],
        out_specs=qspec,
        out_shape=jax.ShapeDtypeStruct((m, width), BF16),
        scratch_shapes=[pltpu.VMEM((2, tq, LANES), F32),
                        pltpu.VMEM((2, tq, LANES), F32)],
        compiler_params=_params("parallel", "parallel", "arbitrary"),
        name="forgetting_attention",
    )(qk_bound, q, k, v, neg_f)


def _pool_kernel(x_ref, w_ref, s_ref, o_ref, pad_ref, *, chunk):
    g = pl.program_id(1)
    seq = x_ref.shape[0]
    pad_ref[0:MAX_WINDOW, :] = jnp.zeros((MAX_WINDOW, LANES), F32)
    pad_ref[MAX_WINDOW:, :] = x_ref[...]
    w = w_ref[...].astype(BF16)
    for gi, win in enumerate(POOL_WINDOWS):
        @pl.when(g == gi)
        def _(win=win):
            for r0 in range(0, seq, chunk):
                x = pad_ref[MAX_WINDOW + r0:MAX_WINDOW + r0 + chunk, :]
                total = x
                for back in range(1, win):
                    total = total + pad_ref[MAX_WINDOW + r0 - back:MAX_WINDOW + r0 - back + chunk, :]
                pos = r0 + lax.broadcasted_iota(jnp.int32, (chunk, LANES), 0)
                count = jnp.minimum(pos + 1, win).astype(F32)
                pooled = total / count - x
                o_ref[r0:r0 + chunk, :] = (_dot(pooled.astype(BF16), w) * s_ref[...]).astype(o_ref.dtype)


def _multiscale_pool(xd, pool_w, pool_scale, batch, seq):
    m, width = xd.shape
    n_groups = width // LANES
    spec = pl.BlockSpec((seq, LANES), lambda b, g: (b, g))
    return pl.pallas_call(
        functools.partial(_pool_kernel, chunk=min(seq, 512)),
        grid=(batch, n_groups),
        in_specs=[spec,
                  pl.BlockSpec((None, LANES, LANES), lambda b, g: (g, 0, 0)),
                  pl.BlockSpec((1, LANES), lambda b, g: (0, g))],
        out_specs=spec,
        out_shape=jax.ShapeDtypeStruct((m, width), BF16),
        scratch_shapes=[pltpu.VMEM((seq + MAX_WINDOW, LANES), F32)],
        compiler_params=_params("parallel", "parallel"),
        name="multiscale_pool",
    )(xd, pool_w, pool_scale)


def _route(logits):
    lane = lax.broadcasted_iota(jnp.int32, logits.shape, 1)
    is_group = lane < N_GROUPS
    gl = jnp.where(is_group, logits, NEG_BIG)
    g_max = jnp.max(gl, axis=1, keepdims=True)
    g_idx = jnp.min(jnp.where(gl == g_max, lane, LANES), axis=1, keepdims=True)
    g_den = jnp.sum(jnp.where(is_group, jnp.exp(gl - g_max), 0.0), axis=1, keepdims=True)
    g_p = 1.0 / g_den
    lo = N_GROUPS + g_idx * EXPERTS_PER_GROUP
    in_group = jnp.logical_and(lane >= lo, lane < lo + EXPERTS_PER_GROUP)
    el = jnp.where(in_group, logits, NEG_BIG)
    top1 = jnp.max(el, axis=1, keepdims=True)
    idx1 = jnp.min(jnp.where(el == top1, lane, LANES), axis=1, keepdims=True)
    el2 = jnp.where(lane == idx1, NEG_BIG, el)
    top2 = jnp.max(el2, axis=1, keepdims=True)
    idx2 = jnp.min(jnp.where(el2 == top2, lane, LANES), axis=1, keepdims=True)
    e2 = jnp.exp(top2 - top1)
    den = 1.0 + e2
    w1 = (1.0 / den) * g_p
    w2 = (e2 / den) * g_p
    return jnp.where(lane == idx1, w1, 0.0) + jnp.where(lane == idx2, w2, 0.0), g_idx


def _moe_kernel(h_ref, ya_ref, yb_ref, wo_ref, g_ref, wr_hi_ref, wr_lo_ref, br_ref, wgu_ref, wd_ref,
                p_ref, wp_ref, gn_ref, wg_ref, o_ref,
                h1_ref, xs_ref, gs_ref, ys_ref, pos_ref, cnt_ref):
    grp = pl.program_id(1)
    t, d = h_ref.shape
    win = MOE_WINDOW
    group_width = EXPERTS_PER_GROUP * D_EXPERT

    @pl.when(grp == 0)
    def _():
        half = ya_ref.shape[1]
        h1 = h_ref[...] + _dot(ya_ref[...], wo_ref[0:half, :]) + _dot(yb_ref[...], wo_ref[half:, :])
        h1_ref[...] = h1
        hn = _rms_normalize(h1) * g_ref[...]
        logits = _dot3(hn, wr_hi_ref[...], wr_lo_ref[...]) + br_ref[...]
        gates, g_idx = _route(logits)
        lane = lax.broadcasted_iota(jnp.int32, (t, LANES), 1)
        onehot = jnp.where(lane == g_idx, 1.0, 0.0)
        r_i = lax.broadcasted_iota(jnp.int32, (t, t), 0)
        c_i = lax.broadcasted_iota(jnp.int32, (t, t), 1)
        earlier = jnp.where(c_i < r_i, 1.0, 0.0).astype(BF16)
        rank = _dot(earlier, onehot.astype(BF16))
        counts = jnp.broadcast_to(rank[t - 1:t, :] + onehot[t - 1:t, :], (8, LANES))
        lr = lax.broadcasted_iota(jnp.int32, (LANES, LANES), 0)
        lc = lax.broadcasted_iota(jnp.int32, (LANES, LANES), 1)
        offsets = _dot_exact_rhs(counts, jnp.where(lr < lc, 1.0, 0.0).astype(BF16))
        pos = jnp.sum(onehot * (offsets[0:1, :] + rank), axis=1, keepdims=True)
        pos_b = jnp.broadcast_to(pos, (t, LANES)).astype(jnp.int32)
        pos_ref[...] = pos_b
        cnt_ref[0:8, :] = counts.astype(jnp.int32)
        cnt_ref[8:16, :] = offsets.astype(jnp.int32)
        pos_row = jnp.transpose(pos_b)[0:1, :]
        perm = jnp.where(r_i == pos_row, 1.0, 0.0).astype(BF16)
        g_hi, g_lo = _split_bf16(gates)
        srt = _dot(perm, jnp.concatenate([hn.astype(BF16), g_hi, g_lo], axis=1))
        xs_ref[0:t, :] = srt[:, :d].astype(BF16)
        xs_ref[t:, :] = jnp.zeros((win, d), BF16)
        gs_ref[0:t, :] = srt[:, d:d + LANES] + srt[:, d + LANES:]
        gs_ref[t:, :] = jnp.zeros((win, LANES), F32)
        ys_ref[...] = jnp.zeros_like(ys_ref)

    lane1 = lax.broadcasted_iota(jnp.int32, (1, LANES), 1)
    count = jnp.sum(jnp.where(lane1 == grp, cnt_ref[0:1, :], 0))
    start = jnp.sum(jnp.where(lane1 == grp, cnt_ref[8:9, :], 0))
    base = (start // BF16_ROWS) * BF16_ROWS
    n_windows = (start - base + count + win - 1) // win

    def window(w, carry):
        r0 = pl.multiple_of(base + w * win, BF16_ROWS)
        xs = xs_ref[pl.ds(r0, win), :]
        gs = gs_ref[pl.ds(r0, win), :]
        gu = _dot(xs, wgu_ref[...])
        hg = gu[:, :group_width]
        hu = gu[:, group_width:]
        act = (hg * jax.nn.sigmoid(hg)) * hu
        lane = lax.broadcasted_iota(jnp.int32, (win, LANES), 1)
        parts = []
        for e in range(EXPERTS_PER_GROUP):
            gate = jnp.sum(jnp.where(lane == N_GROUPS + grp * EXPERTS_PER_GROUP + e, gs, 0.0),
                           axis=1, keepdims=True)
            parts.append((act[:, e * D_EXPERT:(e + 1) * D_EXPERT] * gate).astype(BF16))
        ys_ref[pl.ds(r0, win), :] += _dot(jnp.concatenate(parts, axis=1), wd_ref[...])
        return carry

    lax.fori_loop(0, n_windows, window, 0)

    @pl.when(grp == N_GROUPS - 1)
    def _():
        c_i = lax.broadcasted_iota(jnp.int32, (t, t), 1)
        pos_t = jnp.concatenate([pos_ref[...]] * (t // LANES), axis=1)
        unperm = jnp.where(c_i == pos_t, 1.0, 0.0).astype(BF16)
        h2 = h1_ref[...] + _dot(unperm, ys_ref[0:t, :].astype(BF16))
        emb = _rms_normalize(_dot(p_ref[...].astype(BF16), wp_ref[...])) * gn_ref[...]
        emb_gate = jax.nn.sigmoid(_dot(_rms_normalize(h2).astype(BF16), wg_ref[...]))
        o_ref[...] = h2 + emb_gate * emb


def _moe(h, ya, yb, wo, g, wr_hi, wr_lo, br, wgu, wd, p, wp, gn, wg):
    m, d = h.shape
    rows = lambda a: pl.BlockSpec((MOE_TILE, a.shape[1]), lambda i, e: (i, 0))
    full = lambda a: pl.BlockSpec(a.shape, lambda i, e: (0,) * a.ndim)
    group = lambda a: pl.BlockSpec((None,) + a.shape[1:], lambda i, e: (e, 0, 0))
    return pl.pallas_call(
        _moe_kernel,
        grid=(m // MOE_TILE, N_GROUPS),
        in_specs=[rows(h), rows(ya), rows(yb), full(wo), full(g), full(wr_hi), full(wr_lo), full(br),
                  group(wgu), group(wd), rows(p), full(wp), full(gn), full(wg)],
        out_specs=rows(h),
        out_shape=jax.ShapeDtypeStruct((m, d), F32),
        scratch_shapes=[pltpu.VMEM((MOE_TILE, d), F32),
                        pltpu.VMEM((MOE_TILE + MOE_WINDOW, d), BF16),
                        pltpu.VMEM((MOE_TILE + MOE_WINDOW, LANES), F32),
                        pltpu.VMEM((MOE_TILE + MOE_WINDOW, d), F32),
                        pltpu.VMEM((MOE_TILE, LANES), jnp.int32),
                        pltpu.VMEM((16, LANES), jnp.int32)],
        compiler_params=pltpu.CompilerParams(dimension_semantics=("parallel", "arbitrary"),
                                             vmem_limit_bytes=MOE_VMEM_LIMIT),
        name="outproj_moe_ple",
    )(h, ya, yb, wo, g, wr_hi, wr_lo, br, wgu, wd, p, wp, gn, wg)


def _row(a):
    return a.reshape(1, -1).astype(F32)


def _hi_lo(w):
    hi = w.astype(BF16)
    return hi, (w - hi.astype(F32)).astype(BF16)


def _even_mixer(h, g, w_in, sgu_w, sgu_b, sgu_norm, batch, seq):
    u, va, q, k, vb = _even_in_proj(h, _row(g), w_in.astype(BF16))
    bias = jnp.repeat(sgu_b.T, HEAD_DIM, axis=1)
    ya = _spatial_gating(u, va, sgu_w, bias, _row(sgu_norm))
    yb = _stick_breaking(q, k, vb, batch, seq)
    return ya, yb


def _odd_mixer(h, g, w_in, forget_bias, q_norm, k_norm, pool_w, pool_scale, batch, seq):
    n_heads = forget_bias.shape[0]
    width = n_heads * HEAD_DIM
    w_main = jnp.concatenate([w_in[:, :3 * width], w_in[:, 3 * width + n_heads:]], axis=1).astype(BF16)
    wf_hi, wf_lo = _hi_lo(w_in[:, 3 * width:3 * width + n_heads].T)
    gain = lambda t: jnp.tile(t, LANES // HEAD_DIM).reshape(1, LANES).astype(F32)
    q, k, v, xd, neg_f = _odd_in_proj(h, _row(g), w_main, wf_hi, wf_lo,
                                      forget_bias.reshape(n_heads, 1).astype(F32),
                                      gain(q_norm), gain(k_norm), seq)
    qk_bound = (1.02 * HEAD_DIM * HEAD_DIM ** -0.5 * LOG2_E
                * jnp.max(jnp.abs(q_norm)) * jnp.max(jnp.abs(k_norm))).reshape(1, 1).astype(F32)
    yc = _forgetting_attention(qk_bound, q, k, v, neg_f, batch, seq)
    yd = _multiscale_pool(xd, pool_w, _row(pool_scale), batch, seq)
    return yc, yd


def _moe_layer(h, ya, yb, w_out, g, rg_w, rg_b, re_w, re_b, w_gate, w_up, w_down,
               p, ple_w_proj, ple_norm, ple_w_gate):
    d = h.shape[1]
    n_route = N_GROUPS + N_EXPERTS
    wr = jnp.zeros((d, LANES), F32).at[:, :N_GROUPS].set(rg_w).at[:, N_GROUPS:n_route].set(re_w)
    br = jnp.zeros((1, LANES), F32).at[0, :N_GROUPS].set(rg_b).at[0, N_GROUPS:n_route].set(re_b)
    wr_hi, wr_lo = _hi_lo(wr)
    by_group = lambda w: (w.reshape(N_GROUPS, EXPERTS_PER_GROUP, d, D_EXPERT)
                          .transpose(0, 2, 1, 3).reshape(N_GROUPS, d, EXPERTS_PER_GROUP * D_EXPERT))
    wgu = jnp.concatenate([by_group(w_gate), by_group(w_up)], axis=2).astype(BF16)
    wd = w_down.reshape(N_GROUPS, EXPERTS_PER_GROUP * D_EXPERT, d).astype(BF16)
    return _moe(h, ya, yb, w_out.astype(BF16), _row(g), wr_hi, wr_lo, br, wgu, wd,
                p, ple_w_proj.astype(BF16), _row(ple_norm), ple_w_gate.astype(BF16))


def kernel(x, p, norm_mix, norm_ffn, even_w_in, sgu_w, sgu_b, sgu_norm, even_w_out, odd_w_in, forget_bias, q_norm, k_norm, pool_w, pool_scale, odd_w_out, router_group_w, router_group_b, router_expert_w, router_expert_b, expert_w_gate, expert_w_up, expert_w_down, ple_w_proj, ple_norm, ple_w_gate):
    batch, seq, d = x.shape
    depth = p.shape[0]
    h = x.reshape(batch * seq, d)
    for i in range(depth):
        j = i // 2
        if i % 2 == 0:
            ya, yb = _even_mixer(h, norm_mix[i], even_w_in[j], sgu_w[j], sgu_b[j], sgu_norm[j],
                                 batch, seq)
            w_out = even_w_out[j]
        else:
            ya, yb = _odd_mixer(h, norm_mix[i], odd_w_in[j], forget_bias[j], q_norm[j], k_norm[j],
                                pool_w[j], pool_scale[j], batch, seq)
            w_out = odd_w_out[j]
        h = _moe_layer(h, ya, yb, w_out, norm_ffn[i], router_group_w[i], router_group_b[i],
                       router_expert_w[i], router_expert_b[i], expert_w_gate[i], expert_w_up[i],
                       expert_w_down[i], p[i].reshape(batch * seq, -1), ple_w_proj[i], ple_norm[i],
                       ple_w_gate[i])
    return h.reshape(batch, seq, d)
```

```python
import functools

import jax
import jax.numpy as jnp
from jax import lax
from jax.experimental import pallas as pl
from jax.experimental.pallas import tpu as pltpu

F32 = jnp.float32
BF16 = jnp.bfloat16

HEAD_DIM = 64
LANES = 128
CHUNK = 128
POOL_WINDOWS = (2, 4, 8, 16)
MAX_WINDOW = max(POOL_WINDOWS)
N_GROUPS = 4
EXPERTS_PER_GROUP = 4
N_EXPERTS = N_GROUPS * EXPERTS_PER_GROUP
D_EXPERT = 256
RMS_EPS = 1e-6
ATT_BLOCK = 256
FORGET_Q_BLOCK = 512
FORGET_K_BLOCK = 512
LOG2_E = 1.4426950408889634
ROW_TILE = 512
MOE_TILE = 512
MOE_WINDOW = 176
BF16_ROWS = 16
VMEM_LIMIT = 48 * 1024 * 1024
MOE_VMEM_LIMIT = 56 * 1024 * 1024
EXP2_ZERO_BELOW = -150.0
NEG_BIG = -1e30


def _params(*sem):
    return pltpu.CompilerParams(dimension_semantics=sem, vmem_limit_bytes=VMEM_LIMIT)


def _split_bf16(x):
    hi = x.astype(BF16)
    lo = (x - hi.astype(F32)).astype(BF16)
    return hi, lo


def _dot(a, b):
    return jnp.dot(a, b, preferred_element_type=F32)


def _dot_nt(a, b):
    return lax.dot_general(a, b, (((1,), (1,)), ((), ())), preferred_element_type=F32)


def _dot_exact_rhs(x, m):
    hi, lo = _split_bf16(x)
    return _dot(hi, m) + _dot(lo, m)


def _dot3(x, w_hi, w_lo):
    hi, lo = _split_bf16(x)
    return _dot(hi, w_hi) + (_dot(hi, w_lo) + _dot(lo, w_hi))


def _dot3_nt(w_hi, w_lo, x):
    hi, lo = _split_bf16(x)
    return _dot_nt(w_hi, hi) + (_dot_nt(w_lo, hi) + _dot_nt(w_hi, lo))


def _rms_normalize(x):
    return x * lax.rsqrt(jnp.mean(x * x, axis=-1, keepdims=True) + RMS_EPS)


def _log_sigmoid(x):
    return jnp.minimum(x, 0.0) - jnp.log1p(jnp.exp(-jnp.abs(x)))


def _gelu_tanh(x):
    return 0.5 * x * (1.0 + jnp.tanh(0.7978845608028654 * (x + 0.044715 * (x * x * x))))


def _head_group_matrix():
    r = lax.broadcasted_iota(jnp.int32, (LANES, LANES), 0) // HEAD_DIM
    c = lax.broadcasted_iota(jnp.int32, (LANES, LANES), 1) // HEAD_DIM
    return jnp.where(r == c, 1.0, 0.0).astype(BF16)


def _first_head_mask(shape):
    return lax.broadcasted_iota(jnp.int32, shape, len(shape) - 1) < HEAD_DIM


def _even_in_kernel(h_ref, g_ref, w_ref, u_ref, va_ref, q_ref, k_ref, vb_ref, *, width):
    hn = (_rms_normalize(h_ref[...]) * g_ref[...]).astype(BF16)
    outs = (u_ref, va_ref, q_ref, k_ref, vb_ref)
    for c, o in enumerate(outs):
        r = _dot(hn, w_ref[:, c * width:(c + 1) * width])
        if o is q_ref:
            r = r * (HEAD_DIM ** -0.5 * LOG2_E)
        o[...] = r.astype(o.dtype)


def _even_in_proj(h, g, w):
    m, d = h.shape
    width = w.shape[1] // 5
    out = jax.ShapeDtypeStruct((m, width), BF16)
    row = pl.BlockSpec((ROW_TILE, width), lambda i: (i, 0))
    return pl.pallas_call(
        functools.partial(_even_in_kernel, width=width),
        grid=(m // ROW_TILE,),
        in_specs=[pl.BlockSpec((ROW_TILE, d), lambda i: (i, 0)),
                  pl.BlockSpec((1, d), lambda i: (0, 0)),
                  pl.BlockSpec(w.shape, lambda i: (0, 0))],
        out_specs=[row] * 5,
        out_shape=[out] * 5,
        compiler_params=_params("parallel"),
        name="even_in_proj",
    )(h, g, w)


def _sgu_kernel(u_ref, v_ref, w_ref, b_ref, g_ref, o_ref):
    n_pairs = u_ref.shape[1] // LANES
    n_chunks = u_ref.shape[0] // CHUNK
    gmat = _head_group_matrix()
    first = _first_head_mask((CHUNK, LANES))
    tril = (lax.broadcasted_iota(jnp.int32, (CHUNK, CHUNK), 0)
            >= lax.broadcasted_iota(jnp.int32, (CHUNK, CHUNK), 1))
    for p in range(n_pairs):
        cols = slice(p * LANES, (p + 1) * LANES)
        w_a = jnp.where(tril, w_ref[2 * p], 0.0).astype(BF16)
        w_b = jnp.where(tril, w_ref[2 * p + 1], 0.0).astype(BF16)
        v = _gelu_tanh(v_ref[:, cols].astype(F32))
        ms = _dot((v * v).astype(BF16), gmat) * (1.0 / HEAD_DIM)
        vn = v * lax.rsqrt(ms + RMS_EPS) * g_ref[:, cols]
        for c in range(n_chunks):
            rows = slice(c * CHUNK, (c + 1) * CHUNK)
            vc = vn[rows]
            mixed = (_dot(w_a, jnp.where(first, vc, 0.0).astype(BF16))
                     + _dot(w_b, jnp.where(first, 0.0, vc).astype(BF16)) + b_ref[:, cols])
            u = _gelu_tanh(u_ref[rows, cols].astype(F32))
            o_ref[rows, cols] = (u * mixed).astype(o_ref.dtype)


def _spatial_gating(u, va, w_s, bias, g_v):
    m, width = u.shape
    row = pl.BlockSpec((ROW_TILE, width), lambda i: (i, 0))
    return pl.pallas_call(
        _sgu_kernel,
        grid=(m // ROW_TILE,),
        in_specs=[row, row,
                  pl.BlockSpec(w_s.shape, lambda i: (0, 0, 0)),
                  pl.BlockSpec(bias.shape, lambda i: (0, 0)),
                  pl.BlockSpec(g_v.shape, lambda i: (0, 0))],
        out_specs=row,
        out_shape=jax.ShapeDtypeStruct((m, width), BF16),
        compiler_params=_params("parallel"),
        name="spatial_gating",
    )(u, va, w_s, bias, g_v)


def _stick_kernel(q_ref, k_ref, v_ref, o_ref, acc_ref, r_ref):
    blk = ATT_BLOCK
    n_pairs = q_ref.shape[1] // LANES
    i = pl.program_id(1)
    first_q = _first_head_mask((blk, LANES))
    row = lax.broadcasted_iota(jnp.int32, (blk, blk), 0)
    col = lax.broadcasted_iota(jnp.int32, (blk, blk), 1)
    suffix = jnp.where(row >= col, 1.0, 0.0).astype(BF16)
    suffix2 = jnp.concatenate([suffix, suffix], axis=0)
    strict = col < row
    acc_ref[...] = jnp.zeros_like(acc_ref)
    r_ref[...] = jnp.zeros_like(r_ref)

    def sweep(j, diagonal):
        start = pl.multiple_of(j * blk, blk)
        n_heads = 2 * n_pairs
        zs = []
        for p in range(n_pairs):
            cols = slice(p * LANES, (p + 1) * LANES)
            q = q_ref[:, cols].astype(F32)
            q2 = jnp.concatenate([jnp.where(first_q, q, 0.0), jnp.where(first_q, 0.0, q)], axis=0)
            zs.append(_dot_nt(q2.astype(BF16), k_ref[pl.ds(start, blk), cols]))
        z = jnp.concatenate(zs, axis=0)
        nz = -z
        log_1m_beta = jnp.minimum(nz, 0.0) - jnp.log2(1.0 + jnp.exp2(jnp.minimum(z, nz)))
        if diagonal:
            strict_all = jnp.concatenate([strict] * n_heads, axis=0)
            log_1m_beta = jnp.where(strict_all, log_1m_beta, 0.0)
        hi, lo = _split_bf16(log_1m_beta)
        tail = _dot(jnp.concatenate([hi, lo], axis=1), suffix2)
        r = r_ref[...]
        weights = jnp.exp2(z + tail + jnp.concatenate([r] * (blk // LANES), axis=1))
        if diagonal:
            weights = jnp.where(strict_all, weights, 0.0)
        weights = weights.astype(BF16)
        r_ref[...] = r + jnp.broadcast_to(tail[:, 0:1], r.shape)
        for p in range(n_pairs):
            cols = slice(p * LANES, (p + 1) * LANES)
            v = v_ref[pl.ds(start, blk), cols].astype(F32)
            v2 = jnp.concatenate([jnp.where(first_q, v, 0.0), jnp.where(first_q, 0.0, v)], axis=0)
            w2 = jnp.concatenate([weights[2 * p * blk:(2 * p + 1) * blk],
                                  weights[(2 * p + 1) * blk:(2 * p + 2) * blk]], axis=1)
            acc_ref[p] += _dot(w2, v2.astype(BF16))

    sweep(i, True)

    def cond(carry):
        j, live = carry
        return jnp.logical_and(j >= 0, live)

    def any_live():
        return jnp.max(r_ref[...]) > EXP2_ZERO_BELOW

    def body(carry):
        j, _ = carry
        sweep(j, False)
        return j - 1, any_live()

    lax.while_loop(cond, body, (i - 1, any_live()))
    for p in range(n_pairs):
        o_ref[:, p * LANES:(p + 1) * LANES] = acc_ref[p].astype(o_ref.dtype)


def _stick_breaking(q, k, v, batch, seq):
    m, width = q.shape
    nq = seq // ATT_BLOCK
    n_pairs = width // LANES
    qspec = pl.BlockSpec((ATT_BLOCK, width), lambda b, i: (b * nq + i, 0))
    kvspec = pl.BlockSpec((seq, width), lambda b, i: (b, 0))
    return pl.pallas_call(
        _stick_kernel,
        grid=(batch, nq),
        in_specs=[qspec, kvspec, kvspec],
        out_specs=qspec,
        out_shape=jax.ShapeDtypeStruct((m, width), BF16),
        scratch_shapes=[pltpu.VMEM((n_pairs, ATT_BLOCK, LANES), F32),
                        pltpu.VMEM((2 * n_pairs * ATT_BLOCK, LANES), F32)],
        compiler_params=_params("parallel", "arbitrary"),
        name="stick_breaking",
    )(q, k, v)


def _odd_in_kernel(h_ref, g_ref, w_ref, wf_hi_ref, wf_lo_ref, fb_ref, qn_ref, kn_ref,
                   q_ref, k_ref, v_ref, xd_ref, nf_ref, carry_ref, *, width, tiles_per_seq):
    i = pl.program_id(0)
    hn32 = _rms_normalize(h_ref[...]) * g_ref[...]
    hn = hn32.astype(BF16)
    gmat = _head_group_matrix()

    def head_norm(x, gain_ref):
        parts = []
        for p in range(width // LANES):
            xp = x[:, p * LANES:(p + 1) * LANES]
            ms = _dot((xp * xp).astype(BF16), gmat) * (1.0 / HEAD_DIM)
            parts.append(xp * lax.rsqrt(ms + RMS_EPS) * gain_ref[...])
        return jnp.concatenate(parts, axis=1)

    qf = head_norm(_dot(hn, w_ref[:, 0:width]), qn_ref) * (HEAD_DIM ** -0.5 * LOG2_E)
    q_ref[...] = qf.astype(q_ref.dtype)
    k_ref[...] = head_norm(_dot(hn, w_ref[:, width:2 * width]), kn_ref).astype(k_ref.dtype)
    v_ref[...] = _dot(hn, w_ref[:, 2 * width:3 * width]).astype(v_ref.dtype)
    xd_ref[...] = _dot(hn, w_ref[:, 3 * width:4 * width]).astype(xd_ref.dtype)

    f_logit = _dot3_nt(wf_hi_ref[...], wf_lo_ref[...], hn32)
    log_f = _log_sigmoid(f_logit + fb_ref[...])
    n_heads, tile = log_f.shape

    @pl.when(i % tiles_per_seq == 0)
    def _():
        carry_ref[...] = jnp.zeros_like(carry_ref)

    r = lax.broadcasted_iota(jnp.int32, (LANES, LANES), 0)
    c = lax.broadcasted_iota(jnp.int32, (LANES, LANES), 1)
    prefix = jnp.where(r <= c, 1.0, 0.0).astype(BF16)
    carry = carry_ref[...]
    for s in range(tile // LANES):
        x = log_f[:, s * LANES:(s + 1) * LANES]
        x1 = x.astype(BF16)
        rem = x - x1.astype(F32)
        x2 = rem.astype(BF16)
        x3 = (rem - x2.astype(F32)).astype(BF16)
        cum = _dot(x1, prefix) + (_dot(x2, prefix) + _dot(x3, prefix)) + carry
        nf_ref[:, s * LANES:(s + 1) * LANES] = cum * (-LOG2_E)
        carry = jnp.broadcast_to(cum[:, LANES - 1:LANES], carry.shape)
    carry_ref[...] = carry


def _odd_in_proj(h, g, w, wf_hi, wf_lo, fb, qn, kn, seq):
    m, d = h.shape
    width = w.shape[1] // 4
    n_heads = wf_hi.shape[0]
    row = lambda dt: pl.BlockSpec((ROW_TILE, width), lambda i: (i, 0))
    full = lambda a: pl.BlockSpec(a.shape, lambda i: (0,) * a.ndim)
    return pl.pallas_call(
        functools.partial(_odd_in_kernel, width=width, tiles_per_seq=seq // ROW_TILE),
        grid=(m // ROW_TILE,),
        in_specs=[pl.BlockSpec((ROW_TILE, d), lambda i: (i, 0)),
                  full(g), full(w), full(wf_hi), full(wf_lo), full(fb), full(qn), full(kn)],
        out_specs=[row(BF16), row(BF16), row(BF16), row(F32),
                   pl.BlockSpec((n_heads, ROW_TILE), lambda i: (0, i))],
        out_shape=[jax.ShapeDtypeStruct((m, width), BF16)] * 3
                  + [jax.ShapeDtypeStruct((m, width), F32),
                     jax.ShapeDtypeStruct((n_heads, m), F32)],
        scratch_shapes=[pltpu.VMEM((n_heads, LANES), F32)],
        compiler_params=_params("arbitrary"),
        name="odd_in_proj",
    )(h, g, w, wf_hi, wf_lo, fb, qn, kn)


def _forget_kernel(qk_bound_ref, q_ref, k_ref, v_ref, nf_ref, o_ref, acc_ref, m_ref):
    tq, tk = FORGET_Q_BLOCK, FORGET_K_BLOCK
    i = pl.program_id(2)
    first_q = _first_head_mask((tq, LANES))
    lane = lax.broadcasted_iota(jnp.int32, (tk, LANES), 1)
    first_k = lane < HEAD_DIM
    q = q_ref[...].astype(F32)
    q_heads = (jnp.where(first_q, q, 0.0).astype(BF16), jnp.where(first_q, 0.0, q).astype(BF16))
    den_lane = (HEAD_DIM, 0)
    den_cols = [jnp.where(lane == dl, 1.0, 0.0) for dl in den_lane]
    acc_ref[...] = jnp.zeros_like(acc_ref)
    m_ref[...] = jnp.full_like(m_ref, NEG_BIG)
    diag_block = (i * tq) // tk

    def sweep(j, diagonal):
        start = pl.multiple_of(j * tk, tk)
        k = k_ref[pl.ds(start, tk), :]
        v = v_ref[pl.ds(start, tk), :].astype(F32)
        v_heads = (jnp.where(first_k, v, den_cols[0]).astype(BF16),
                   jnp.where(first_k, den_cols[1], v).astype(BF16))
        if diagonal:
            q_pos = i * tq + lax.broadcasted_iota(jnp.int32, (tq, tk), 0)
            k_pos = start + lax.broadcasted_iota(jnp.int32, (tq, tk), 1)
            causal = k_pos <= q_pos
        m_prevs = [m_ref[hd] for hd in range(2)]
        accs = [acc_ref[hd] for hd in range(2)]
        m_news = []
        for hd in range(2):
            s = _dot_nt(q_heads[hd], k) + nf_ref[hd:hd + 1, pl.ds(start, tk)]
            if diagonal:
                s = jnp.where(causal, s, NEG_BIG)
            m_prev = m_prevs[hd]
            m_new = jnp.maximum(m_prev, jnp.broadcast_to(jnp.max(s, axis=1, keepdims=True), m_prev.shape))
            p = jnp.exp2(s - jnp.concatenate([m_new] * (tk // LANES), axis=1))
            accs[hd] = jnp.exp2(m_prev - m_new) * accs[hd] + _dot(p.astype(BF16), v_heads[hd])
            m_news.append(m_new)
        for hd in range(2):
            acc_ref[hd] = accs[hd]
            m_ref[hd] = m_news[hd]

    def block_live(j):
        tail = pl.multiple_of(jnp.maximum(j, 0) * tk + tk - LANES, LANES)
        live = False
        for hd in range(2):
            top = qk_bound_ref[0, 0] + jnp.max(nf_ref[hd:hd + 1, pl.ds(tail, LANES)], axis=1, keepdims=True)
            live = jnp.logical_or(live, jnp.max(top - m_ref[hd]) > EXP2_ZERO_BELOW)
        return live

    sweep(diag_block, True)

    def cond(carry):
        j, live = carry
        return jnp.logical_and(j >= 0, live)

    def body(carry):
        j, _ = carry
        sweep(j, False)
        return j - 1, block_live(j - 1)

    lax.while_loop(cond, body, (diag_block - 1, block_live(diag_block - 1)))

    acc_a = acc_ref[0]
    acc_b = acc_ref[1]
    den_a = jnp.broadcast_to(acc_a[:, den_lane[0]:den_lane[0] + 1], acc_a.shape)
    den_b = jnp.broadcast_to(acc_b[:, den_lane[1]:den_lane[1] + 1], acc_b.shape)
    o_ref[...] = jnp.where(first_q, acc_a / den_a, acc_b / den_b).astype(o_ref.dtype)


def _forgetting_attention(qk_bound, q, k, v, neg_f, batch, seq):
    m, width = q.shape
    tq = FORGET_Q_BLOCK
    assert FORGET_K_BLOCK % tq == 0 and seq % FORGET_K_BLOCK == 0
    nq = seq // tq
    n_pairs = width // LANES
    neg_f = neg_f.reshape(n_pairs, 2, m)
    qspec = pl.BlockSpec((tq, LANES), lambda b, p, i: (b * nq + i, p))
    kvspec = pl.BlockSpec((seq, LANES), lambda b, p, i: (b, p))
    return pl.pallas_call(
        _forget_kernel,
        grid=(batch, n_pairs, nq),
        in_specs=[pl.BlockSpec(memory_space=pltpu.SMEM), qspec, kvspec, kvspec,
                  pl.BlockSpec((None, 2, seq), lambda b, p, i: (p, 0, b))],
        out_specs=qspec,
        out_shape=jax.ShapeDtypeStruct((m, width), BF16),
        scratch_shapes=[pltpu.VMEM((2, tq, LANES), F32),
                        pltpu.VMEM((2, tq, LANES), F32)],
        compiler_params=_params("parallel", "parallel", "arbitrary"),
        name="forgetting_attention",
    )(qk_bound, q, k, v, neg_f)


def _pool_kernel(x_ref, w_ref, s_ref, o_ref, pad_ref, *, chunk):
    g = pl.program_id(1)
    seq = x_ref.shape[0]
    pad_ref[0:MAX_WINDOW, :] = jnp.zeros((MAX_WINDOW, LANES), F32)
    pad_ref[MAX_WINDOW:, :] = x_ref[...]
    w = w_ref[...].astype(BF16)
    for gi, win in enumerate(POOL_WINDOWS):
        @pl.when(g == gi)
        def _(win=win):
            for r0 in range(0, seq, chunk):
                x = pad_ref[MAX_WINDOW + r0:MAX_WINDOW + r0 + chunk, :]
                total = x
                for back in range(1, win):
                    total = total + pad_ref[MAX_WINDOW + r0 - back:MAX_WINDOW + r0 - back + chunk, :]
                pos = r0 + lax.broadcasted_iota(jnp.int32, (chunk, LANES), 0)
                count = jnp.minimum(pos + 1, win).astype(F32)
                pooled = total / count - x
                o_ref[r0:r0 + chunk, :] = (_dot(pooled.astype(BF16), w) * s_ref[...]).astype(o_ref.dtype)


def _multiscale_pool(xd, pool_w, pool_scale, batch, seq):
    m, width = xd.shape
    n_groups = width // LANES
    spec = pl.BlockSpec((seq, LANES), lambda b, g: (b, g))
    return pl.pallas_call(
        functools.partial(_pool_kernel, chunk=min(seq, 512)),
        grid=(batch, n_groups),
        in_specs=[spec,
                  pl.BlockSpec((None, LANES, LANES), lambda b, g: (g, 0, 0)),
                  pl.BlockSpec((1, LANES), lambda b, g: (0, g))],
        out_specs=spec,
        out_shape=jax.ShapeDtypeStruct((m, width), BF16),
        scratch_shapes=[pltpu.VMEM((seq + MAX_WINDOW, LANES), F32)],
        compiler_params=_params("parallel", "parallel"),
        name="multiscale_pool",
    )(xd, pool_w, pool_scale)


def _route(logits):
    lane = lax.broadcasted_iota(jnp.int32, logits.shape, 1)
    is_group = lane < N_GROUPS
    gl = jnp.where(is_group, logits, NEG_BIG)
    g_max = jnp.max(gl, axis=1, keepdims=True)
    g_idx = jnp.min(jnp.where(gl == g_max, lane, LANES), axis=1, keepdims=True)
    g_den = jnp.sum(jnp.where(is_group, jnp.exp(gl - g_max), 0.0), axis=1, keepdims=True)
    g_p = 1.0 / g_den
    lo = N_GROUPS + g_idx * EXPERTS_PER_GROUP
    in_group = jnp.logical_and(lane >= lo, lane < lo + EXPERTS_PER_GROUP)
    el = jnp.where(in_group, logits, NEG_BIG)
    top1 = jnp.max(el, axis=1, keepdims=True)
    idx1 = jnp.min(jnp.where(el == top1, lane, LANES), axis=1, keepdims=True)
    el2 = jnp.where(lane == idx1, NEG_BIG, el)
    top2 = jnp.max(el2, axis=1, keepdims=True)
    idx2 = jnp.min(jnp.where(el2 == top2, lane, LANES), axis=1, keepdims=True)
    e2 = jnp.exp(top2 - top1)
    den = 1.0 + e2
    w1 = (1.0 / den) * g_p
    w2 = (e2 / den) * g_p
    return jnp.where(lane == idx1, w1, 0.0) + jnp.where(lane == idx2, w2, 0.0), g_idx


def _moe_kernel(h_ref, ya_ref, yb_ref, wo_ref, g_ref, wr_hi_ref, wr_lo_ref, br_ref, wgu_ref, wd_ref,
                p_ref, wp_ref, gn_ref, wg_ref, o_ref,
                h1_ref, xs_ref, gs_ref, ys_ref, pos_ref, cnt_ref):
    t, d = h_ref.shape
    win = MOE_WINDOW
    group_width = EXPERTS_PER_GROUP * D_EXPERT

    def sort_by_group():
        half = ya_ref.shape[1]
        h1 = h_ref[...] + _dot(ya_ref[...], wo_ref[0:half, :]) + _dot(yb_ref[...], wo_ref[half:, :])
        h1_ref[...] = h1
        hn = _rms_normalize(h1) * g_ref[...]
        logits = _dot3(hn, wr_hi_ref[...], wr_lo_ref[...]) + br_ref[...]
        gates, g_idx = _route(logits)
        lane = lax.broadcasted_iota(jnp.int32, (t, LANES), 1)
        onehot = jnp.where(lane == g_idx, 1.0, 0.0)
        r_i = lax.broadcasted_iota(jnp.int32, (t, t), 0)
        c_i = lax.broadcasted_iota(jnp.int32, (t, t), 1)
        earlier = jnp.where(c_i < r_i, 1.0, 0.0).astype(BF16)
        rank = _dot(earlier, onehot.astype(BF16))
        counts = jnp.broadcast_to(rank[t - 1:t, :] + onehot[t - 1:t, :], (8, LANES))
        lr = lax.broadcasted_iota(jnp.int32, (LANES, LANES), 0)
        lc = lax.broadcasted_iota(jnp.int32, (LANES, LANES), 1)
        offsets = _dot_exact_rhs(counts, jnp.where(lr < lc, 1.0, 0.0).astype(BF16))
        pos = jnp.sum(onehot * (offsets[0:1, :] + rank), axis=1, keepdims=True)
        pos_b = jnp.broadcast_to(pos, (t, LANES)).astype(jnp.int32)
        pos_ref[...] = pos_b
        cnt_ref[0:8, :] = counts.astype(jnp.int32)
        cnt_ref[8:16, :] = offsets.astype(jnp.int32)
        pos_row = jnp.transpose(pos_b)[0:1, :]
        perm = jnp.where(r_i == pos_row, 1.0, 0.0).astype(BF16)
        g_hi, g_lo = _split_bf16(gates)
        srt = _dot(perm, jnp.concatenate([hn.astype(BF16), g_hi, g_lo], axis=1))
        xs_ref[0:t, :] = srt[:, :d].astype(BF16)
        xs_ref[t:, :] = jnp.zeros((win, d), BF16)
        gs_ref[0:t, :] = srt[:, d:d + LANES] + srt[:, d + LANES:]
        gs_ref[t:, :] = jnp.zeros((win, LANES), F32)
        ys_ref[...] = jnp.zeros_like(ys_ref)

    def run_group(grp, carry):
        lane1 = lax.broadcasted_iota(jnp.int32, (1, LANES), 1)
        count = jnp.sum(jnp.where(lane1 == grp, cnt_ref[0:1, :], 0))
        start = jnp.sum(jnp.where(lane1 == grp, cnt_ref[8:9, :], 0))
        base = (start // BF16_ROWS) * BF16_ROWS
        n_windows = (start - base + count + win - 1) // win

        def window(w, c):
            r0 = pl.multiple_of(base + w * win, BF16_ROWS)
            xs = xs_ref[pl.ds(r0, win), :]
            gs = gs_ref[pl.ds(r0, win), :]
            gu = _dot(xs, wgu_ref[grp])
            hg = gu[:, :group_width]
            hu = gu[:, group_width:]
            act = (hg * jax.nn.sigmoid(hg)) * hu
            lane = lax.broadcasted_iota(jnp.int32, (win, LANES), 1)
            parts = []
            for e in range(EXPERTS_PER_GROUP):
                gate = jnp.sum(jnp.where(lane == N_GROUPS + grp * EXPERTS_PER_GROUP + e, gs, 0.0),
                               axis=1, keepdims=True)
                parts.append((act[:, e * D_EXPERT:(e + 1) * D_EXPERT] * gate).astype(BF16))
            ys_ref[pl.ds(r0, win), :] += _dot(jnp.concatenate(parts, axis=1), wd_ref[grp])
            return c

        return lax.fori_loop(0, n_windows, window, carry)

    def unsort_and_embed():
        c_i = lax.broadcasted_iota(jnp.int32, (t, t), 1)
        pos_t = jnp.concatenate([pos_ref[...]] * (t // LANES), axis=1)
        unperm = jnp.where(c_i == pos_t, 1.0, 0.0).astype(BF16)
        h2 = h1_ref[...] + _dot(unperm, ys_ref[0:t, :].astype(BF16))
        emb = _rms_normalize(_dot(p_ref[...].astype(BF16), wp_ref[...])) * gn_ref[...]
        emb_gate = jax.nn.sigmoid(_dot(_rms_normalize(h2).astype(BF16), wg_ref[...]))
        o_ref[...] = h2 + emb_gate * emb

    sort_by_group()
    lax.fori_loop(0, N_GROUPS, run_group, 0)
    unsort_and_embed()


def _moe(h, ya, yb, wo, g, wr_hi, wr_lo, br, wgu, wd, p, wp, gn, wg):
    m, d = h.shape
    rows = lambda a: pl.BlockSpec((MOE_TILE, a.shape[1]), lambda i: (i, 0))
    full = lambda a: pl.BlockSpec(a.shape, lambda i: (0,) * a.ndim, pipeline_mode=pl.Buffered(1))
    return pl.pallas_call(
        _moe_kernel,
        grid=(m // MOE_TILE,),
        in_specs=[rows(h), rows(ya), rows(yb), full(wo), full(g), full(wr_hi), full(wr_lo), full(br),
                  full(wgu), full(wd), rows(p), full(wp), full(gn), full(wg)],
        out_specs=rows(h),
        out_shape=jax.ShapeDtypeStruct((m, d), F32),
        scratch_shapes=[pltpu.VMEM((MOE_TILE, d), F32),
                        pltpu.VMEM((MOE_TILE + MOE_WINDOW, d), BF16),
                        pltpu.VMEM((MOE_TILE + MOE_WINDOW, LANES), F32),
                        pltpu.VMEM((MOE_TILE + MOE_WINDOW, d), F32),
                        pltpu.VMEM((MOE_TILE, LANES), jnp.int32),
                        pltpu.VMEM((16, LANES), jnp.int32)],
        compiler_params=pltpu.CompilerParams(dimension_semantics=("parallel",),
                                             vmem_limit_bytes=MOE_VMEM_LIMIT),
        name="outproj_moe_ple",
    )(h, ya, yb, wo, g, wr_hi, wr_lo, br, wgu, wd, p, wp, gn, wg)


def _row(a):
    return a.reshape(1, -1).astype(F32)


def _hi_lo(w):
    hi = w.astype(BF16)
    return hi, (w - hi.astype(F32)).astype(BF16)


def _even_mixer(h, g, w_in, sgu_w, sgu_b, sgu_norm, batch, seq):
    u, va, q, k, vb = _even_in_proj(h, _row(g), w_in.astype(BF16))
    bias = jnp.repeat(sgu_b.T, HEAD_DIM, axis=1)
    ya = _spatial_gating(u, va, sgu_w, bias, _row(sgu_norm))
    yb = _stick_breaking(q, k, vb, batch, seq)
    return ya, yb


def _odd_mixer(h, g, w_in, forget_bias, q_norm, k_norm, pool_w, pool_scale, w_out, batch, seq):
    n_heads = forget_bias.shape[0]
    width = n_heads * HEAD_DIM
    order = jnp.argsort(forget_bias)
    cols = (order[:, None] * HEAD_DIM + jnp.arange(HEAD_DIM)[None, :]).reshape(-1)
    w_main = jnp.concatenate([w_in[:, c * width:(c + 1) * width][:, cols] for c in range(3)]
                             + [w_in[:, 3 * width + n_heads:]], axis=1).astype(BF16)
    wf_hi, wf_lo = _hi_lo(w_in[:, 3 * width:3 * width + n_heads][:, order].T)
    forget_bias = forget_bias[order]
    w_out = jnp.concatenate([w_out[:width][cols], w_out[width:]], axis=0)
    gain = lambda t: jnp.tile(t, LANES // HEAD_DIM).reshape(1, LANES).astype(F32)
    q, k, v, xd, neg_f = _odd_in_proj(h, _row(g), w_main, wf_hi, wf_lo,
                                      forget_bias.reshape(n_heads, 1).astype(F32),
                                      gain(q_norm), gain(k_norm), seq)
    qk_bound = (1.02 * HEAD_DIM * HEAD_DIM ** -0.5 * LOG2_E
                * jnp.max(jnp.abs(q_norm)) * jnp.max(jnp.abs(k_norm))).reshape(1, 1).astype(F32)
    yc = _forgetting_attention(qk_bound, q, k, v, neg_f, batch, seq)
    yd = _multiscale_pool(xd, pool_w, _row(pool_scale), batch, seq)
    return yc, yd, w_out


def _moe_layer(h, ya, yb, w_out, g, rg_w, rg_b, re_w, re_b, w_gate, w_up, w_down,
               p, ple_w_proj, ple_norm, ple_w_gate):
    d = h.shape[1]
    n_route = N_GROUPS + N_EXPERTS
    wr = jnp.zeros((d, LANES), F32).at[:, :N_GROUPS].set(rg_w).at[:, N_GROUPS:n_route].set(re_w)
    br = jnp.zeros((1, LANES), F32).at[0, :N_GROUPS].set(rg_b).at[0, N_GROUPS:n_route].set(re_b)
    wr_hi, wr_lo = _hi_lo(wr)
    by_group = lambda w: (w.reshape(N_GROUPS, EXPERTS_PER_GROUP, d, D_EXPERT)
                          .transpose(0, 2, 1, 3).reshape(N_GROUPS, d, EXPERTS_PER_GROUP * D_EXPERT))
    wgu = jnp.concatenate([by_group(w_gate), by_group(w_up)], axis=2).astype(BF16)
    wd = w_down.reshape(N_GROUPS, EXPERTS_PER_GROUP * D_EXPERT, d).astype(BF16)
    return _moe(h, ya, yb, w_out.astype(BF16), _row(g), wr_hi, wr_lo, br, wgu, wd,
                p, ple_w_proj.astype(BF16), _row(ple_norm), ple_w_gate.astype(BF16))


def kernel(x, p, norm_mix, norm_ffn, even_w_in, sgu_w, sgu_b, sgu_norm, even_w_out, odd_w_in, forget_bias, q_norm, k_norm, pool_w, pool_scale, odd_w_out, router_group_w, router_group_b, router_expert_w, router_expert_b, expert_w_gate, expert_w_up, expert_w_down, ple_w_proj, ple_norm, ple_w_gate):
    batch, seq, d = x.shape
    depth = p.shape[0]
    h = x.reshape(batch * seq, d)
    for i in range(depth):
        j = i // 2
        if i % 2 == 0:
            ya, yb = _even_mixer(h, norm_mix[i], even_w_in[j], sgu_w[j], sgu_b[j], sgu_norm[j],
                                 batch, seq)
            w_out = even_w_out[j]
        else:
            ya, yb, w_out = _odd_mixer(h, norm_mix[i], odd_w_in[j], forget_bias[j], q_norm[j],
                                       k_norm[j], pool_w[j], pool_scale[j], odd_w_out[j], batch, seq)
        h = _moe_layer(h, ya, yb, w_out, norm_ffn[i], router_group_w[i], router_group_b[i],
                       router_expert_w[i], router_expert_b[i], expert_w_gate[i], expert_w_up[i],
                       expert_w_down[i], p[i].reshape(batch * seq, -1), ple_w_proj[i], ple_norm[i],
                       ple_w_gate[i])
    return h.reshape(batch, seq, d)
```

```python
import functools

import jax
import jax.numpy as jnp
from jax import lax
from jax.experimental import pallas as pl
from jax.experimental.pallas import tpu as pltpu

F32 = jnp.float32
BF16 = jnp.bfloat16

HEAD_DIM = 64
LANES = 128
CHUNK = 128
POOL_WINDOWS = (2, 4, 8, 16)
MAX_WINDOW = max(POOL_WINDOWS)
N_GROUPS = 4
EXPERTS_PER_GROUP = 4
N_EXPERTS = N_GROUPS * EXPERTS_PER_GROUP
D_EXPERT = 256
RMS_EPS = 1e-6
ATT_BLOCK = 256
FORGET_Q_BLOCK = 512
FORGET_K_BLOCK = 512
LOG2_E = 1.4426950408889634
ROW_TILE = 512
MOE_TILE = 512
MOE_WINDOW = 176
BF16_ROWS = 16
VMEM_LIMIT = 48 * 1024 * 1024
MOE_VMEM_LIMIT = 56 * 1024 * 1024
EXP2_ZERO_BELOW = -150.0
NEG_BIG = -1e30


def _params(*sem):
    return pltpu.CompilerParams(dimension_semantics=sem, vmem_limit_bytes=VMEM_LIMIT)


def _split_bf16(x):
    hi = x.astype(BF16)
    lo = (x - hi.astype(F32)).astype(BF16)
    return hi, lo


def _dot(a, b):
    return jnp.dot(a, b, preferred_element_type=F32)


def _dot_nt(a, b):
    return lax.dot_general(a, b, (((1,), (1,)), ((), ())), preferred_element_type=F32)


def _dot_exact_rhs(x, m):
    hi, lo = _split_bf16(x)
    return _dot(hi, m) + _dot(lo, m)


def _dot3(x, w_hi, w_lo):
    hi, lo = _split_bf16(x)
    return _dot(hi, w_hi) + (_dot(hi, w_lo) + _dot(lo, w_hi))


def _dot3_nt(w_hi, w_lo, x):
    hi, lo = _split_bf16(x)
    return _dot_nt(w_hi, hi) + (_dot_nt(w_lo, hi) + _dot_nt(w_hi, lo))


def _rms_normalize(x):
    return x * lax.rsqrt(jnp.mean(x * x, axis=-1, keepdims=True) + RMS_EPS)


def _log_sigmoid(x):
    return jnp.minimum(x, 0.0) - jnp.log1p(jnp.exp(-jnp.abs(x)))


def _gelu_tanh(x):
    return 0.5 * x * (1.0 + jnp.tanh(0.7978845608028654 * (x + 0.044715 * (x * x * x))))


def _head_group_matrix():
    r = lax.broadcasted_iota(jnp.int32, (LANES, LANES), 0) // HEAD_DIM
    c = lax.broadcasted_iota(jnp.int32, (LANES, LANES), 1) // HEAD_DIM
    return jnp.where(r == c, 1.0, 0.0).astype(BF16)


def _first_head_mask(shape):
    return lax.broadcasted_iota(jnp.int32, shape, len(shape) - 1) < HEAD_DIM


def _even_in_kernel(h_ref, g_ref, w_ref, ws_ref, bs_ref, gv_ref, ya_ref, q_ref, k_ref, vb_ref, *, width):
    hn = (_rms_normalize(h_ref[...]) * g_ref[...]).astype(BF16)
    proj = lambda c: _dot(hn, w_ref[:, c * width:(c + 1) * width])
    q_ref[...] = (proj(2) * (HEAD_DIM ** -0.5 * LOG2_E)).astype(q_ref.dtype)
    k_ref[...] = proj(3).astype(k_ref.dtype)
    vb_ref[...] = proj(4).astype(vb_ref.dtype)

    u_all = _gelu_tanh(proj(0))
    v_all = _gelu_tanh(proj(1))
    n_chunks = h_ref.shape[0] // CHUNK
    gmat = _head_group_matrix()
    first = _first_head_mask((CHUNK, LANES))
    tril = (lax.broadcasted_iota(jnp.int32, (CHUNK, CHUNK), 0)
            >= lax.broadcasted_iota(jnp.int32, (CHUNK, CHUNK), 1))
    for p in range(width // LANES):
        cols = slice(p * LANES, (p + 1) * LANES)
        w_a = jnp.where(tril, ws_ref[2 * p], 0.0).astype(BF16)
        w_b = jnp.where(tril, ws_ref[2 * p + 1], 0.0).astype(BF16)
        v = v_all[:, cols]
        ms = _dot((v * v).astype(BF16), gmat) * (1.0 / HEAD_DIM)
        vn = v * lax.rsqrt(ms + RMS_EPS) * gv_ref[:, cols]
        for c in range(n_chunks):
            rows = slice(c * CHUNK, (c + 1) * CHUNK)
            vc = vn[rows]
            mixed = (_dot(w_a, jnp.where(first, vc, 0.0).astype(BF16))
                     + _dot(w_b, jnp.where(first, 0.0, vc).astype(BF16)) + bs_ref[:, cols])
            ya_ref[rows, cols] = (u_all[rows, cols] * mixed).astype(ya_ref.dtype)


def _even_in_proj(h, g, w, w_s, bias, g_v):
    m, d = h.shape
    width = w.shape[1] // 5
    out = jax.ShapeDtypeStruct((m, width), BF16)
    row = pl.BlockSpec((ROW_TILE, width), lambda i: (i, 0))
    full = lambda a: pl.BlockSpec(a.shape, lambda i: (0,) * a.ndim)
    return pl.pallas_call(
        functools.partial(_even_in_kernel, width=width),
        grid=(m // ROW_TILE,),
        in_specs=[pl.BlockSpec((ROW_TILE, d), lambda i: (i, 0)),
                  full(g), full(w), full(w_s), full(bias), full(g_v)],
        out_specs=[row] * 4,
        out_shape=[out] * 4,
        compiler_params=_params("parallel"),
        name="even_in_proj",
    )(h, g, w, w_s, bias, g_v)


def _stick_kernel(q_ref, k_ref, v_ref, o_ref, acc_ref, r_ref):
    blk = ATT_BLOCK
    n_pairs = q_ref.shape[1] // LANES
    i = pl.program_id(1)
    first_q = _first_head_mask((blk, LANES))
    row = lax.broadcasted_iota(jnp.int32, (blk, blk), 0)
    col = lax.broadcasted_iota(jnp.int32, (blk, blk), 1)
    suffix = jnp.where(row >= col, 1.0, 0.0).astype(BF16)
    suffix2 = jnp.concatenate([suffix, suffix], axis=0)
    strict = col < row
    acc_ref[...] = jnp.zeros_like(acc_ref)
    r_ref[...] = jnp.zeros_like(r_ref)

    def sweep(j, diagonal):
        start = pl.multiple_of(j * blk, blk)
        n_heads = 2 * n_pairs
        zs = []
        for p in range(n_pairs):
            cols = slice(p * LANES, (p + 1) * LANES)
            q = q_ref[:, cols].astype(F32)
            q2 = jnp.concatenate([jnp.where(first_q, q, 0.0), jnp.where(first_q, 0.0, q)], axis=0)
            zs.append(_dot_nt(q2.astype(BF16), k_ref[pl.ds(start, blk), cols]))
        z = jnp.concatenate(zs, axis=0)
        nz = -z
        log_1m_beta = jnp.minimum(nz, 0.0) - jnp.log2(1.0 + jnp.exp2(jnp.minimum(z, nz)))
        if diagonal:
            strict_all = jnp.concatenate([strict] * n_heads, axis=0)
            log_1m_beta = jnp.where(strict_all, log_1m_beta, 0.0)
        hi, lo = _split_bf16(log_1m_beta)
        tail = _dot(jnp.concatenate([hi, lo], axis=1), suffix2)
        r = r_ref[...]
        weights = jnp.exp2(z + tail + jnp.concatenate([r] * (blk // LANES), axis=1))
        if diagonal:
            weights = jnp.where(strict_all, weights, 0.0)
        weights = weights.astype(BF16)
        r_ref[...] = r + jnp.broadcast_to(tail[:, 0:1], r.shape)
        for p in range(n_pairs):
            cols = slice(p * LANES, (p + 1) * LANES)
            v = v_ref[pl.ds(start, blk), cols].astype(F32)
            v2 = jnp.concatenate([jnp.where(first_q, v, 0.0), jnp.where(first_q, 0.0, v)], axis=0)
            w2 = jnp.concatenate([weights[2 * p * blk:(2 * p + 1) * blk],
                                  weights[(2 * p + 1) * blk:(2 * p + 2) * blk]], axis=1)
            acc_ref[p] += _dot(w2, v2.astype(BF16))

    sweep(i, True)

    def cond(carry):
        j, live = carry
        return jnp.logical_and(j >= 0, live)

    def any_live():
        return jnp.max(r_ref[...]) > EXP2_ZERO_BELOW

    def body(carry):
        j, _ = carry
        sweep(j, False)
        return j - 1, any_live()

    lax.while_loop(cond, body, (i - 1, any_live()))
    for p in range(n_pairs):
        o_ref[:, p * LANES:(p + 1) * LANES] = acc_ref[p].astype(o_ref.dtype)


def _stick_breaking(q, k, v, batch, seq):
    m, width = q.shape
    nq = seq // ATT_BLOCK
    n_pairs = width // LANES
    qspec = pl.BlockSpec((ATT_BLOCK, width), lambda b, i: (b * nq + i, 0))
    kvspec = pl.BlockSpec((seq, width), lambda b, i: (b, 0))
    return pl.pallas_call(
        _stick_kernel,
        grid=(batch, nq),
        in_specs=[qspec, kvspec, kvspec],
        out_specs=qspec,
        out_shape=jax.ShapeDtypeStruct((m, width), BF16),
        scratch_shapes=[pltpu.VMEM((n_pairs, ATT_BLOCK, LANES), F32),
                        pltpu.VMEM((2 * n_pairs * ATT_BLOCK, LANES), F32)],
        compiler_params=_params("parallel", "arbitrary"),
        name="stick_breaking",
    )(q, k, v)


def _odd_in_kernel(h_ref, g_ref, w_ref, wf_hi_ref, wf_lo_ref, fb_ref, qn_ref, kn_ref,
                   q_ref, k_ref, v_ref, xd_ref, nf_ref, carry_ref, *, width, tiles_per_seq):
    i = pl.program_id(0)
    hn32 = _rms_normalize(h_ref[...]) * g_ref[...]
    hn = hn32.astype(BF16)
    gmat = _head_group_matrix()

    def head_norm(x, gain_ref):
        parts = []
        for p in range(width // LANES):
            xp = x[:, p * LANES:(p + 1) * LANES]
            ms = _dot((xp * xp).astype(BF16), gmat) * (1.0 / HEAD_DIM)
            parts.append(xp * lax.rsqrt(ms + RMS_EPS) * gain_ref[...])
        return jnp.concatenate(parts, axis=1)

    qf = head_norm(_dot(hn, w_ref[:, 0:width]), qn_ref) * (HEAD_DIM ** -0.5 * LOG2_E)
    q_ref[...] = qf.astype(q_ref.dtype)
    k_ref[...] = head_norm(_dot(hn, w_ref[:, width:2 * width]), kn_ref).astype(k_ref.dtype)
    v_ref[...] = _dot(hn, w_ref[:, 2 * width:3 * width]).astype(v_ref.dtype)
    xd_ref[...] = _dot(hn, w_ref[:, 3 * width:4 * width]).astype(xd_ref.dtype)

    f_logit = _dot3_nt(wf_hi_ref[...], wf_lo_ref[...], hn32)
    log_f = _log_sigmoid(f_logit + fb_ref[...])
    n_heads, tile = log_f.shape

    @pl.when(i % tiles_per_seq == 0)
    def _():
        carry_ref[...] = jnp.zeros_like(carry_ref)

    r = lax.broadcasted_iota(jnp.int32, (LANES, LANES), 0)
    c = lax.broadcasted_iota(jnp.int32, (LANES, LANES), 1)
    prefix = jnp.where(r <= c, 1.0, 0.0).astype(BF16)
    carry = carry_ref[...]
    for s in range(tile // LANES):
        x = log_f[:, s * LANES:(s + 1) * LANES]
        x1 = x.astype(BF16)
        rem = x - x1.astype(F32)
        x2 = rem.astype(BF16)
        x3 = (rem - x2.astype(F32)).astype(BF16)
        cum = _dot(x1, prefix) + (_dot(x2, prefix) + _dot(x3, prefix)) + carry
        nf_ref[:, s * LANES:(s + 1) * LANES] = cum * (-LOG2_E)
        carry = jnp.broadcast_to(cum[:, LANES - 1:LANES], carry.shape)
    carry_ref[...] = carry


def _odd_in_proj(h, g, w, wf_hi, wf_lo, fb, qn, kn, seq):
    m, d = h.shape
    width = w.shape[1] // 4
    n_heads = wf_hi.shape[0]
    row = lambda dt: pl.BlockSpec((ROW_TILE, width), lambda i: (i, 0))
    full = lambda a: pl.BlockSpec(a.shape, lambda i: (0,) * a.ndim)
    return pl.pallas_call(
        functools.partial(_odd_in_kernel, width=width, tiles_per_seq=seq // ROW_TILE),
        grid=(m // ROW_TILE,),
        in_specs=[pl.BlockSpec((ROW_TILE, d), lambda i: (i, 0)),
                  full(g), full(w), full(wf_hi), full(wf_lo), full(fb), full(qn), full(kn)],
        out_specs=[row(BF16), row(BF16), row(BF16), row(F32),
                   pl.BlockSpec((n_heads, ROW_TILE), lambda i: (0, i))],
        out_shape=[jax.ShapeDtypeStruct((m, width), BF16)] * 3
                  + [jax.ShapeDtypeStruct((m, width), F32),
                     jax.ShapeDtypeStruct((n_heads, m), F32)],
        scratch_shapes=[pltpu.VMEM((n_heads, LANES), F32)],
        compiler_params=_params("arbitrary"),
        name="odd_in_proj",
    )(h, g, w, wf_hi, wf_lo, fb, qn, kn)


def _forget_kernel(qk_bound_ref, q_ref, k_ref, v_ref, nf_ref, o_ref, acc_ref, m_ref):
    tq, tk = FORGET_Q_BLOCK, FORGET_K_BLOCK
    i = pl.program_id(2)
    first_q = _first_head_mask((tq, LANES))
    lane = lax.broadcasted_iota(jnp.int32, (tk, LANES), 1)
    first_k = lane < HEAD_DIM
    q = q_ref[...].astype(F32)
    q2 = jnp.concatenate([jnp.where(first_q, q, 0.0), jnp.where(first_q, 0.0, q)], axis=0).astype(BF16)
    den_lane = (HEAD_DIM, 0)
    den_cols = [jnp.where(lane == dl, 1.0, 0.0) for dl in den_lane]
    acc_ref[...] = jnp.zeros_like(acc_ref)
    m_ref[...] = jnp.full_like(m_ref, NEG_BIG)
    diag_block = (i * tq) // tk

    def sweep(j, diagonal):
        start = pl.multiple_of(j * tk, tk)
        k = k_ref[pl.ds(start, tk), :]
        v = v_ref[pl.ds(start, tk), :].astype(F32)
        v_heads = (jnp.where(first_k, v, den_cols[0]).astype(BF16),
                   jnp.where(first_k, den_cols[1], v).astype(BF16))
        if diagonal:
            q_pos = i * tq + lax.broadcasted_iota(jnp.int32, (tq, tk), 0)
            k_pos = start + lax.broadcasted_iota(jnp.int32, (tq, tk), 1)
            causal = k_pos <= q_pos
        s2 = _dot_nt(q2, k)
        s_heads = [s2[hd * tq:(hd + 1) * tq] + nf_ref[hd:hd + 1, pl.ds(start, tk)] for hd in range(2)]
        if diagonal:
            s_heads = [jnp.where(causal, s, NEG_BIG) for s in s_heads]
        s = jnp.concatenate(s_heads, axis=0)
        m_prev = m_ref[...]
        m_new = jnp.maximum(m_prev, jnp.broadcast_to(jnp.max(s, axis=1, keepdims=True), m_prev.shape))
        p = jnp.exp2(s - jnp.concatenate([m_new] * (tk // LANES), axis=1)).astype(BF16)
        pv = jnp.concatenate([_dot(p[hd * tq:(hd + 1) * tq], v_heads[hd]) for hd in range(2)], axis=0)
        acc_ref[...] = jnp.exp2(m_prev - m_new) * acc_ref[...] + pv
        m_ref[...] = m_new

    def block_live(j):
        tail = pl.multiple_of(jnp.maximum(j, 0) * tk + tk - LANES, LANES)
        live = False
        for hd in range(2):
            top = qk_bound_ref[0, 0] + jnp.max(nf_ref[hd:hd + 1, pl.ds(tail, LANES)], axis=1, keepdims=True)
            live = jnp.logical_or(live, jnp.max(top - m_ref[hd * tq:(hd + 1) * tq, :]) > EXP2_ZERO_BELOW)
        return live

    sweep(diag_block, True)

    def cond(carry):
        j, live = carry
        return jnp.logical_and(j >= 0, live)

    def body(carry):
        j, _ = carry
        sweep(j, False)
        return j - 1, block_live(j - 1)

    lax.while_loop(cond, body, (diag_block - 1, block_live(diag_block - 1)))

    acc_a = acc_ref[0:tq, :]
    acc_b = acc_ref[tq:, :]
    den_a = jnp.broadcast_to(acc_a[:, den_lane[0]:den_lane[0] + 1], acc_a.shape)
    den_b = jnp.broadcast_to(acc_b[:, den_lane[1]:den_lane[1] + 1], acc_b.shape)
    o_ref[...] = jnp.where(first_q, acc_a / den_a, acc_b / den_b).astype(o_ref.dtype)


def _forgetting_attention(qk_bound, q, k, v, neg_f, batch, seq):
    m, width = q.shape
    tq = FORGET_Q_BLOCK
    assert FORGET_K_BLOCK % tq == 0 and seq % FORGET_K_BLOCK == 0
    nq = seq // tq
    n_pairs = width // LANES
    neg_f = neg_f.reshape(n_pairs, 2, m)
    qspec = pl.BlockSpec((tq, LANES), lambda b, p, i: (b * nq + i, p))
    kvspec = pl.BlockSpec((seq, LANES), lambda b, p, i: (b, p))
    return pl.pallas_call(
        _forget_kernel,
        grid=(batch, n_pairs, nq),
        in_specs=[pl.BlockSpec(memory_space=pltpu.SMEM), qspec, kvspec, kvspec,
                  pl.BlockSpec((None, 2, seq), lambda b, p, i: (p, 0, b))],
        out_specs=qspec,
        out_shape=jax.ShapeDtypeStruct((m, width), BF16),
        scratch_shapes=[pltpu.VMEM((2 * tq, LANES), F32),
                        pltpu.VMEM((2 * tq, LANES), F32)],
        compiler_params=_params("parallel", "parallel", "arbitrary"),
        name="forgetting_attention",
    )(qk_bound, q, k, v, neg_f)


def _pool_kernel(x_ref, w_ref, s_ref, o_ref, pad_ref, *, chunk):
    g = pl.program_id(1)
    seq = x_ref.shape[0]
    pad_ref[0:MAX_WINDOW, :] = jnp.zeros((MAX_WINDOW, LANES), F32)
    pad_ref[MAX_WINDOW:, :] = x_ref[...]
    w = w_ref[...].astype(BF16)
    for gi, win in enumerate(POOL_WINDOWS):
        @pl.when(g == gi)
        def _(win=win):
            for r0 in range(0, seq, chunk):
                x = pad_ref[MAX_WINDOW + r0:MAX_WINDOW + r0 + chunk, :]
                total = x
                for back in range(1, win):
                    total = total + pad_ref[MAX_WINDOW + r0 - back:MAX_WINDOW + r0 - back + chunk, :]
                pos = r0 + lax.broadcasted_iota(jnp.int32, (chunk, LANES), 0)
                count = jnp.minimum(pos + 1, win).astype(F32)
                pooled = total / count - x
                o_ref[r0:r0 + chunk, :] = (_dot(pooled.astype(BF16), w) * s_ref[...]).astype(o_ref.dtype)


def _multiscale_pool(xd, pool_w, pool_scale, batch, seq):
    m, width = xd.shape
    n_groups = width // LANES
    spec = pl.BlockSpec((seq, LANES), lambda b, g: (b, g))
    return pl.pallas_call(
        functools.partial(_pool_kernel, chunk=min(seq, 512)),
        grid=(batch, n_groups),
        in_specs=[spec,
                  pl.BlockSpec((None, LANES, LANES), lambda b, g: (g, 0, 0)),
                  pl.BlockSpec((1, LANES), lambda b, g: (0, g))],
        out_specs=spec,
        out_shape=jax.ShapeDtypeStruct((m, width), BF16),
        scratch_shapes=[pltpu.VMEM((seq + MAX_WINDOW, LANES), F32)],
        compiler_params=_params("parallel", "parallel"),
        name="multiscale_pool",
    )(xd, pool_w, pool_scale)


def _route(logits):
    lane = lax.broadcasted_iota(jnp.int32, logits.shape, 1)
    is_group = lane < N_GROUPS
    gl = jnp.where(is_group, logits, NEG_BIG)
    g_max = jnp.max(gl, axis=1, keepdims=True)
    g_idx = jnp.min(jnp.where(gl == g_max, lane, LANES), axis=1, keepdims=True)
    g_den = jnp.sum(jnp.where(is_group, jnp.exp(gl - g_max), 0.0), axis=1, keepdims=True)
    g_p = 1.0 / g_den
    lo = N_GROUPS + g_idx * EXPERTS_PER_GROUP
    in_group = jnp.logical_and(lane >= lo, lane < lo + EXPERTS_PER_GROUP)
    el = jnp.where(in_group, logits, NEG_BIG)
    top1 = jnp.max(el, axis=1, keepdims=True)
    idx1 = jnp.min(jnp.where(el == top1, lane, LANES), axis=1, keepdims=True)
    el2 = jnp.where(lane == idx1, NEG_BIG, el)
    top2 = jnp.max(el2, axis=1, keepdims=True)
    idx2 = jnp.min(jnp.where(el2 == top2, lane, LANES), axis=1, keepdims=True)
    e2 = jnp.exp(top2 - top1)
    den = 1.0 + e2
    w1 = (1.0 / den) * g_p
    w2 = (e2 / den) * g_p
    return jnp.where(lane == idx1, w1, 0.0) + jnp.where(lane == idx2, w2, 0.0), g_idx


def _moe_kernel(h_ref, ya_ref, yb_ref, wo_ref, g_ref, wr_hi_ref, wr_lo_ref, br_ref,
                wgate_ref, wup_ref, wd_ref, p_ref, wp_ref, gn_ref, wg_ref, o_ref,
                h1_ref, xs_ref, gs_ref, ys_ref, pos_ref, cnt_ref):
    t, d = h_ref.shape
    win = MOE_WINDOW

    def sort_by_group():
        half = ya_ref.shape[1]
        h1 = h_ref[...] + _dot(ya_ref[...], wo_ref[0:half, :]) + _dot(yb_ref[...], wo_ref[half:, :])
        h1_ref[...] = h1
        hn = _rms_normalize(h1) * g_ref[...]
        logits = _dot3(hn, wr_hi_ref[...], wr_lo_ref[...]) + br_ref[...]
        gates, g_idx = _route(logits)
        lane = lax.broadcasted_iota(jnp.int32, (t, LANES), 1)
        onehot = jnp.where(lane == g_idx, 1.0, 0.0)
        r_i = lax.broadcasted_iota(jnp.int32, (t, t), 0)
        c_i = lax.broadcasted_iota(jnp.int32, (t, t), 1)
        earlier = jnp.where(c_i < r_i, 1.0, 0.0).astype(BF16)
        rank = _dot(earlier, onehot.astype(BF16))
        counts = jnp.broadcast_to(rank[t - 1:t, :] + onehot[t - 1:t, :], (8, LANES))
        lr = lax.broadcasted_iota(jnp.int32, (LANES, LANES), 0)
        lc = lax.broadcasted_iota(jnp.int32, (LANES, LANES), 1)
        offsets = _dot_exact_rhs(counts, jnp.where(lr < lc, 1.0, 0.0).astype(BF16))
        pos = jnp.sum(onehot * (offsets[0:1, :] + rank), axis=1, keepdims=True)
        pos_b = jnp.broadcast_to(pos, (t, LANES)).astype(jnp.int32)
        pos_ref[...] = pos_b
        cnt_ref[0:8, :] = counts.astype(jnp.int32)
        cnt_ref[8:16, :] = offsets.astype(jnp.int32)
        pos_row = jnp.transpose(pos_b)[0:1, :]
        perm = jnp.where(r_i == pos_row, 1.0, 0.0).astype(BF16)
        g_hi, g_lo = _split_bf16(gates)
        srt = _dot(perm, jnp.concatenate([hn.astype(BF16), g_hi, g_lo], axis=1))
        xs_ref[0:t, :] = srt[:, :d].astype(BF16)
        xs_ref[t:, :] = jnp.zeros((win, d), BF16)
        gs_ref[0:t, :] = srt[:, d:d + LANES] + srt[:, d + LANES:]
        gs_ref[t:, :] = jnp.zeros((win, LANES), F32)
        ys_ref[...] = jnp.zeros_like(ys_ref)

    def run_group(grp, carry):
        lane1 = lax.broadcasted_iota(jnp.int32, (1, LANES), 1)
        count = jnp.sum(jnp.where(lane1 == grp, cnt_ref[0:1, :], 0))
        start = jnp.sum(jnp.where(lane1 == grp, cnt_ref[8:9, :], 0))
        base = (start // BF16_ROWS) * BF16_ROWS
        n_windows = (start - base + count + win - 1) // win

        def window(w, c):
            r0 = pl.multiple_of(base + w * win, BF16_ROWS)
            xs = xs_ref[pl.ds(r0, win), :]
            gs = gs_ref[pl.ds(r0, win), :]
            lane = lax.broadcasted_iota(jnp.int32, (win, LANES), 1)
            parts = []
            for e in range(EXPERTS_PER_GROUP):
                expert = grp * EXPERTS_PER_GROUP + e
                hg = _dot(xs, wgate_ref[expert])
                hu = _dot(xs, wup_ref[expert])
                gate = jnp.sum(jnp.where(lane == N_GROUPS + expert, gs, 0.0), axis=1, keepdims=True)
                parts.append(((hg * jax.nn.sigmoid(hg)) * hu * gate).astype(BF16))
            ys_ref[pl.ds(r0, win), :] += _dot(jnp.concatenate(parts, axis=1), wd_ref[grp])
            return c

        return lax.fori_loop(0, n_windows, window, carry)

    def unsort_and_embed():
        c_i = lax.broadcasted_iota(jnp.int32, (t, t), 1)
        pos_t = jnp.concatenate([pos_ref[...]] * (t // LANES), axis=1)
        unperm = jnp.where(c_i == pos_t, 1.0, 0.0).astype(BF16)
        h2 = h1_ref[...] + _dot(unperm, ys_ref[0:t, :].astype(BF16))
        emb = _rms_normalize(_dot(p_ref[...].astype(BF16), wp_ref[...])) * gn_ref[...]
        emb_gate = jax.nn.sigmoid(_dot(_rms_normalize(h2).astype(BF16), wg_ref[...]))
        o_ref[...] = h2 + emb_gate * emb

    sort_by_group()
    lax.fori_loop(0, N_GROUPS, run_group, 0)
    unsort_and_embed()


def _moe(h, ya, yb, wo, g, wr_hi, wr_lo, br, wgate, wup, wd, p, wp, gn, wg):
    m, d = h.shape
    rows = lambda a: pl.BlockSpec((MOE_TILE, a.shape[1]), lambda i: (i, 0))
    full = lambda a: pl.BlockSpec(a.shape, lambda i: (0,) * a.ndim, pipeline_mode=pl.Buffered(1))
    return pl.pallas_call(
        _moe_kernel,
        grid=(m // MOE_TILE,),
        in_specs=[rows(h), rows(ya), rows(yb), full(wo), full(g), full(wr_hi), full(wr_lo), full(br),
                  full(wgate), full(wup), full(wd), rows(p), full(wp), full(gn), full(wg)],
        out_specs=rows(h),
        out_shape=jax.ShapeDtypeStruct((m, d), F32),
        scratch_shapes=[pltpu.VMEM((MOE_TILE, d), F32),
                        pltpu.VMEM((MOE_TILE + MOE_WINDOW, d), BF16),
                        pltpu.VMEM((MOE_TILE + MOE_WINDOW, LANES), F32),
                        pltpu.VMEM((MOE_TILE + MOE_WINDOW, d), F32),
                        pltpu.VMEM((MOE_TILE, LANES), jnp.int32),
                        pltpu.VMEM((16, LANES), jnp.int32)],
        compiler_params=pltpu.CompilerParams(dimension_semantics=("parallel",),
                                             vmem_limit_bytes=MOE_VMEM_LIMIT),
        name="outproj_moe_ple",
    )(h, ya, yb, wo, g, wr_hi, wr_lo, br, wgate, wup, wd, p, wp, gn, wg)


def _row(a):
    return a.reshape(1, -1).astype(F32)


def _hi_lo(w):
    hi = w.astype(BF16)
    return hi, (w - hi.astype(F32)).astype(BF16)


def _even_mixer(h, g, w_in, sgu_w, sgu_b, sgu_norm, batch, seq):
    bias = jnp.repeat(sgu_b.T, HEAD_DIM, axis=1)
    ya, q, k, vb = _even_in_proj(h, _row(g), w_in.astype(BF16), sgu_w, bias, _row(sgu_norm))
    yb = _stick_breaking(q, k, vb, batch, seq)
    return ya, yb


def _odd_mixer(h, g, w_in, forget_bias, q_norm, k_norm, pool_w, pool_scale, w_out, batch, seq):
    n_heads = forget_bias.shape[0]
    width = n_heads * HEAD_DIM
    order = jnp.argsort(forget_bias)
    cols = (order[:, None] * HEAD_DIM + jnp.arange(HEAD_DIM)[None, :]).reshape(-1)
    w_main = jnp.concatenate([w_in[:, c * width:(c + 1) * width][:, cols] for c in range(3)]
                             + [w_in[:, 3 * width + n_heads:]], axis=1).astype(BF16)
    wf_hi, wf_lo = _hi_lo(w_in[:, 3 * width:3 * width + n_heads][:, order].T)
    forget_bias = forget_bias[order]
    w_out = jnp.concatenate([w_out[:width][cols], w_out[width:]], axis=0)
    gain = lambda t: jnp.tile(t, LANES // HEAD_DIM).reshape(1, LANES).astype(F32)
    q, k, v, xd, neg_f = _odd_in_proj(h, _row(g), w_main, wf_hi, wf_lo,
                                      forget_bias.reshape(n_heads, 1).astype(F32),
                                      gain(q_norm), gain(k_norm), seq)
    qk_bound = (1.02 * HEAD_DIM * HEAD_DIM ** -0.5 * LOG2_E
                * jnp.max(jnp.abs(q_norm)) * jnp.max(jnp.abs(k_norm))).reshape(1, 1).astype(F32)
    yc = _forgetting_attention(qk_bound, q, k, v, neg_f, batch, seq)
    yd = _multiscale_pool(xd, pool_w, _row(pool_scale), batch, seq)
    return yc, yd, w_out


def _moe_layer(h, ya, yb, w_out, g, rg_w, rg_b, re_w, re_b, w_gate, w_up, w_down,
               p, ple_w_proj, ple_norm, ple_w_gate):
    d = h.shape[1]
    n_route = N_GROUPS + N_EXPERTS
    wr = jnp.zeros((d, LANES), F32).at[:, :N_GROUPS].set(rg_w).at[:, N_GROUPS:n_route].set(re_w)
    br = jnp.zeros((1, LANES), F32).at[0, :N_GROUPS].set(rg_b).at[0, N_GROUPS:n_route].set(re_b)
    wr_hi, wr_lo = _hi_lo(wr)
    wd = w_down.reshape(N_GROUPS, EXPERTS_PER_GROUP * D_EXPERT, d).astype(BF16)
    return _moe(h, ya, yb, w_out.astype(BF16), _row(g), wr_hi, wr_lo, br,
                w_gate.astype(BF16), w_up.astype(BF16), wd,
                p, ple_w_proj.astype(BF16), _row(ple_norm), ple_w_gate.astype(BF16))


def kernel(x, p, norm_mix, norm_ffn, even_w_in, sgu_w, sgu_b, sgu_norm, even_w_out, odd_w_in, forget_bias, q_norm, k_norm, pool_w, pool_scale, odd_w_out, router_group_w, router_group_b, router_expert_w, router_expert_b, expert_w_gate, expert_w_up, expert_w_down, ple_w_proj, ple_norm, ple_w_gate):
    batch, seq, d = x.shape
    depth = p.shape[0]
    h = x.reshape(batch * seq, d)
    for i in range(depth):
        j = i // 2
        if i % 2 == 0:
            ya, yb = _even_mixer(h, norm_mix[i], even_w_in[j], sgu_w[j], sgu_b[j], sgu_norm[j],
                                 batch, seq)
            w_out = even_w_out[j]
        else:
            ya, yb, w_out = _odd_mixer(h, norm_mix[i], odd_w_in[j], forget_bias[j], q_norm[j],
                                       k_norm[j], pool_w[j], pool_scale[j], odd_w_out[j], batch, seq)
        h = _moe_layer(h, ya, yb, w_out, norm_ffn[i], router_group_w[i], router_group_b[i],
                       router_expert_w[i], router_expert_b[i], expert_w_gate[i], expert_w_up[i],
                       expert_w_down[i], p[i].reshape(batch * seq, -1), ple_w_proj[i], ple_norm[i],
                       ple_w_gate[i])
    return h.reshape(batch, seq, d)
```

```python
import functools

import jax
import jax.numpy as jnp
from jax import lax
from jax.experimental import pallas as pl
from jax.experimental.pallas import tpu as pltpu

F32 = jnp.float32
BF16 = jnp.bfloat16

HEAD_DIM = 64
LANES = 128
CHUNK = 128
POOL_WINDOWS = (2, 4, 8, 16)
MAX_WINDOW = max(POOL_WINDOWS)
N_GROUPS = 4
EXPERTS_PER_GROUP = 4
N_EXPERTS = N_GROUPS * EXPERTS_PER_GROUP
D_EXPERT = 256
RMS_EPS = 1e-6
ATT_BLOCK = 256
FORGET_Q_BLOCK = 512
FORGET_K_BLOCK = 512
LOG2_E = 1.4426950408889634
ROW_TILE = 512
MOE_TILE = 512
MOE_WINDOW = 176
BF16_ROWS = 16
VMEM_LIMIT = 48 * 1024 * 1024
MOE_VMEM_LIMIT = 56 * 1024 * 1024
EXP2_ZERO_BELOW = -150.0
NEG_BIG = -1e30


def _params(*sem):
    return pltpu.CompilerParams(dimension_semantics=sem, vmem_limit_bytes=VMEM_LIMIT)


def _split_bf16(x):
    hi = x.astype(BF16)
    lo = (x - hi.astype(F32)).astype(BF16)
    return hi, lo


def _dot(a, b):
    return jnp.dot(a, b, preferred_element_type=F32)


def _dot_nt(a, b):
    return lax.dot_general(a, b, (((1,), (1,)), ((), ())), preferred_element_type=F32)


def _dot_exact_rhs(x, m):
    hi, lo = _split_bf16(x)
    return _dot(hi, m) + _dot(lo, m)


def _rms_normalize(x):
    return x * lax.rsqrt(jnp.mean(x * x, axis=-1, keepdims=True) + RMS_EPS)


def _log_sigmoid(x):
    return jnp.minimum(x, 0.0) - jnp.log1p(jnp.exp(-jnp.abs(x)))


def _gelu_tanh(x):
    return 0.5 * x * (1.0 + jnp.tanh(0.7978845608028654 * (x + 0.044715 * (x * x * x))))


def _head_group_matrix():
    r = lax.broadcasted_iota(jnp.int32, (LANES, LANES), 0) // HEAD_DIM
    c = lax.broadcasted_iota(jnp.int32, (LANES, LANES), 1) // HEAD_DIM
    return jnp.where(r == c, 1.0, 0.0).astype(BF16)


def _first_head_mask(shape):
    return lax.broadcasted_iota(jnp.int32, shape, len(shape) - 1) < HEAD_DIM


def _even_in_kernel(h_ref, g_ref, w_ref, ws_ref, bs_ref, gv_ref, ya_ref, q_ref, k_ref, vb_ref, *, width):
    hn = (_rms_normalize(h_ref[...]) * g_ref[...]).astype(BF16)
    proj = lambda c: _dot(hn, w_ref[:, c * width:(c + 1) * width])
    q_ref[...] = (proj(2) * (HEAD_DIM ** -0.5 * LOG2_E)).astype(q_ref.dtype)
    k_ref[...] = proj(3).astype(k_ref.dtype)
    vb_ref[...] = proj(4).astype(vb_ref.dtype)

    u_all = _gelu_tanh(proj(0))
    v_all = _gelu_tanh(proj(1))
    n_chunks = h_ref.shape[0] // CHUNK
    gmat = _head_group_matrix()
    first = _first_head_mask((CHUNK, LANES))
    tril = (lax.broadcasted_iota(jnp.int32, (CHUNK, CHUNK), 0)
            >= lax.broadcasted_iota(jnp.int32, (CHUNK, CHUNK), 1))
    for p in range(width // LANES):
        cols = slice(p * LANES, (p + 1) * LANES)
        w_a = jnp.where(tril, ws_ref[2 * p], 0.0).astype(BF16)
        w_b = jnp.where(tril, ws_ref[2 * p + 1], 0.0).astype(BF16)
        v = v_all[:, cols]
        ms = _dot((v * v).astype(BF16), gmat) * (1.0 / HEAD_DIM)
        vn = v * lax.rsqrt(ms + RMS_EPS) * gv_ref[:, cols]
        for c in range(n_chunks):
            rows = slice(c * CHUNK, (c + 1) * CHUNK)
            vc = vn[rows]
            mixed = (_dot(w_a, jnp.where(first, vc, 0.0).astype(BF16))
                     + _dot(w_b, jnp.where(first, 0.0, vc).astype(BF16)) + bs_ref[:, cols])
            ya_ref[rows, cols] = (u_all[rows, cols] * mixed).astype(ya_ref.dtype)


def _even_in_proj(h, g, w, w_s, bias, g_v):
    m, d = h.shape
    width = w.shape[1] // 5
    out = jax.ShapeDtypeStruct((m, width), BF16)
    row = pl.BlockSpec((ROW_TILE, width), lambda i: (i, 0))
    full = lambda a: pl.BlockSpec(a.shape, lambda i: (0,) * a.ndim)
    return pl.pallas_call(
        functools.partial(_even_in_kernel, width=width),
        grid=(m // ROW_TILE,),
        in_specs=[pl.BlockSpec((ROW_TILE, d), lambda i: (i, 0)),
                  full(g), full(w), full(w_s), full(bias), full(g_v)],
        out_specs=[row] * 4,
        out_shape=[out] * 4,
        compiler_params=_params("parallel"),
        name="even_in_proj",
    )(h, g, w, w_s, bias, g_v)


def _stick_kernel(q_ref, k_ref, v_ref, o_ref, acc_ref, r_ref):
    blk = ATT_BLOCK
    n_pairs = q_ref.shape[1] // LANES
    i = pl.program_id(1)
    first_q = _first_head_mask((blk, LANES))
    row = lax.broadcasted_iota(jnp.int32, (blk, blk), 0)
    col = lax.broadcasted_iota(jnp.int32, (blk, blk), 1)
    suffix = jnp.where(row >= col, 1.0, 0.0).astype(BF16)
    suffix2 = jnp.concatenate([suffix, suffix], axis=0)
    strict = col < row
    acc_ref[...] = jnp.zeros_like(acc_ref)
    r_ref[...] = jnp.zeros_like(r_ref)

    def sweep(j, diagonal):
        start = pl.multiple_of(j * blk, blk)
        n_heads = 2 * n_pairs
        zs = []
        for p in range(n_pairs):
            cols = slice(p * LANES, (p + 1) * LANES)
            q = q_ref[:, cols].astype(F32)
            q2 = jnp.concatenate([jnp.where(first_q, q, 0.0), jnp.where(first_q, 0.0, q)], axis=0)
            zs.append(_dot_nt(q2.astype(BF16), k_ref[pl.ds(start, blk), cols]))
        z = jnp.concatenate(zs, axis=0)
        nz = -z
        log_1m_beta = jnp.minimum(nz, 0.0) - jnp.log2(1.0 + jnp.exp2(jnp.minimum(z, nz)))
        if diagonal:
            strict_all = jnp.concatenate([strict] * n_heads, axis=0)
            log_1m_beta = jnp.where(strict_all, log_1m_beta, 0.0)
        hi, lo = _split_bf16(log_1m_beta)
        tail = _dot(jnp.concatenate([hi, lo], axis=1), suffix2)
        r = r_ref[...]
        weights = jnp.exp2(z + tail + jnp.concatenate([r] * (blk // LANES), axis=1))
        if diagonal:
            weights = jnp.where(strict_all, weights, 0.0)
        weights = weights.astype(BF16)
        r_ref[...] = r + jnp.broadcast_to(tail[:, 0:1], r.shape)
        for p in range(n_pairs):
            cols = slice(p * LANES, (p + 1) * LANES)
            v = v_ref[pl.ds(start, blk), cols].astype(F32)
            v2 = jnp.concatenate([jnp.where(first_q, v, 0.0), jnp.where(first_q, 0.0, v)], axis=0)
            w2 = jnp.concatenate([weights[2 * p * blk:(2 * p + 1) * blk],
                                  weights[(2 * p + 1) * blk:(2 * p + 2) * blk]], axis=1)
            acc_ref[p] += _dot(w2, v2.astype(BF16))

    sweep(i, True)

    def cond(carry):
        j, live = carry
        return jnp.logical_and(j >= 0, live)

    def any_live():
        return jnp.max(r_ref[...]) > EXP2_ZERO_BELOW

    def body(carry):
        j, _ = carry
        sweep(j, False)
        return j - 1, any_live()

    lax.while_loop(cond, body, (i - 1, any_live()))
    for p in range(n_pairs):
        o_ref[:, p * LANES:(p + 1) * LANES] = acc_ref[p].astype(o_ref.dtype)


def _stick_breaking(q, k, v, batch, seq):
    m, width = q.shape
    nq = seq // ATT_BLOCK
    n_pairs = width // LANES
    qspec = pl.BlockSpec((ATT_BLOCK, width), lambda b, i: (b * nq + i, 0))
    kvspec = pl.BlockSpec((seq, width), lambda b, i: (b, 0))
    return pl.pallas_call(
        _stick_kernel,
        grid=(batch, nq),
        in_specs=[qspec, kvspec, kvspec],
        out_specs=qspec,
        out_shape=jax.ShapeDtypeStruct((m, width), BF16),
        scratch_shapes=[pltpu.VMEM((n_pairs, ATT_BLOCK, LANES), F32),
                        pltpu.VMEM((2 * n_pairs * ATT_BLOCK, LANES), F32)],
        compiler_params=_params("parallel", "arbitrary"),
        name="stick_breaking",
    )(q, k, v)


def _odd_in_kernel(h_ref, g_ref, w_ref, wf_ref, fb_ref, qn_ref, kn_ref,
                   q_ref, k_ref, v_ref, xd_ref, nf_ref, carry_ref, *, width, tiles_per_seq):
    i = pl.program_id(0)

    @pl.when(i % tiles_per_seq == 0)
    def _():
        carry_ref[...] = jnp.zeros_like(carry_ref)

    hn32 = _rms_normalize(h_ref[...]) * g_ref[...]
    hn, hn_lo = _split_bf16(hn32)
    gmat = _head_group_matrix()

    def head_norm(x, gain_ref):
        parts = []
        for p in range(width // LANES):
            xp = x[:, p * LANES:(p + 1) * LANES]
            ms = _dot((xp * xp).astype(BF16), gmat) * (1.0 / HEAD_DIM)
            parts.append(xp * lax.rsqrt(ms + RMS_EPS) * gain_ref[...])
        return jnp.concatenate(parts, axis=1)

    n_heads = fb_ref.shape[0]
    f_t = jnp.transpose(_dot(hn, wf_ref[...]) + _dot(hn_lo, wf_ref[...]))
    f_logit = f_t[0:n_heads] + f_t[n_heads:2 * n_heads]

    qf = head_norm(_dot(hn, w_ref[:, 0:width]), qn_ref) * (HEAD_DIM ** -0.5 * LOG2_E)
    q_ref[...] = qf.astype(q_ref.dtype)

    log_f = _log_sigmoid(f_logit + fb_ref[...])
    n_heads, tile = log_f.shape
    r = lax.broadcasted_iota(jnp.int32, (LANES, LANES), 0)
    c = lax.broadcasted_iota(jnp.int32, (LANES, LANES), 1)
    prefix = jnp.where(r <= c, 1.0, 0.0).astype(BF16)
    local = []
    for s in range(tile // LANES):
        x = log_f[:, s * LANES:(s + 1) * LANES]
        x1 = x.astype(BF16)
        rem = x - x1.astype(F32)
        x2 = rem.astype(BF16)
        x3 = (rem - x2.astype(F32)).astype(BF16)
        local.append(_dot(x1, prefix) + (_dot(x2, prefix) + _dot(x3, prefix)))

    k_ref[...] = head_norm(_dot(hn, w_ref[:, width:2 * width]), kn_ref).astype(k_ref.dtype)

    carry = carry_ref[...]
    for s in range(tile // LANES):
        cum = local[s] + carry
        nf_ref[:, s * LANES:(s + 1) * LANES] = cum * (-LOG2_E)
        carry = jnp.broadcast_to(cum[:, LANES - 1:LANES], carry.shape)
    carry_ref[...] = carry

    v_ref[...] = _dot(hn, w_ref[:, 2 * width:3 * width]).astype(v_ref.dtype)
    xd_ref[...] = _dot(hn, w_ref[:, 3 * width:4 * width]).astype(xd_ref.dtype)


def _odd_in_proj(h, g, w, wf, fb, qn, kn, seq):
    m, d = h.shape
    width = w.shape[1] // 4
    n_heads = fb.shape[0]
    row = lambda dt: pl.BlockSpec((ROW_TILE, width), lambda i: (i, 0))
    full = lambda a: pl.BlockSpec(a.shape, lambda i: (0,) * a.ndim)
    return pl.pallas_call(
        functools.partial(_odd_in_kernel, width=width, tiles_per_seq=seq // ROW_TILE),
        grid=(m // ROW_TILE,),
        in_specs=[pl.BlockSpec((ROW_TILE, d), lambda i: (i, 0)),
                  full(g), full(w), full(wf), full(fb), full(qn), full(kn)],
        out_specs=[row(BF16), row(BF16), row(BF16), row(F32),
                   pl.BlockSpec((n_heads, ROW_TILE), lambda i: (0, i))],
        out_shape=[jax.ShapeDtypeStruct((m, width), BF16)] * 3
                  + [jax.ShapeDtypeStruct((m, width), F32),
                     jax.ShapeDtypeStruct((n_heads, m), F32)],
        scratch_shapes=[pltpu.VMEM((n_heads, LANES), F32)],
        compiler_params=_params("arbitrary"),
        name="odd_in_proj",
    )(h, g, w, wf, fb, qn, kn)


def _forget_kernel(qk_bound_ref, q_ref, k_ref, v_ref, nf_ref, o_ref, acc_ref, m_ref):
    tq, tk = FORGET_Q_BLOCK, FORGET_K_BLOCK
    i = pl.program_id(2)
    first_q = _first_head_mask((tq, LANES))
    lane = lax.broadcasted_iota(jnp.int32, (tk, LANES), 1)
    first_k = lane < HEAD_DIM
    q = q_ref[...].astype(F32)
    q2 = jnp.concatenate([jnp.where(first_q, q, 0.0), jnp.where(first_q, 0.0, q)], axis=0).astype(BF16)
    den_lane = (HEAD_DIM, 0)
    den_cols = [jnp.where(lane == dl, 1.0, 0.0) for dl in den_lane]
    acc_ref[...] = jnp.zeros_like(acc_ref)
    m_ref[...] = jnp.full_like(m_ref, NEG_BIG)
    diag_block = (i * tq) // tk

    def sweep(j, diagonal):
        start = pl.multiple_of(j * tk, tk)
        k = k_ref[pl.ds(start, tk), :]
        v = v_ref[pl.ds(start, tk), :].astype(F32)
        v_heads = (jnp.where(first_k, v, den_cols[0]).astype(BF16),
                   jnp.where(first_k, den_cols[1], v).astype(BF16))
        if diagonal:
            q_pos = i * tq + lax.broadcasted_iota(jnp.int32, (tq, tk), 0)
            k_pos = start + lax.broadcasted_iota(jnp.int32, (tq, tk), 1)
            causal = k_pos <= q_pos
        s2 = _dot_nt(q2, k)
        s_heads = [s2[hd * tq:(hd + 1) * tq] + nf_ref[hd:hd + 1, pl.ds(start, tk)] for hd in range(2)]
        if diagonal:
            s_heads = [jnp.where(causal, s, NEG_BIG) for s in s_heads]
        s = jnp.concatenate(s_heads, axis=0)
        m_prev = m_ref[...]
        m_new = jnp.maximum(m_prev, jnp.broadcast_to(jnp.max(s, axis=1, keepdims=True), m_prev.shape))
        p = jnp.exp2(s - jnp.concatenate([m_new] * (tk // LANES), axis=1)).astype(BF16)
        pv = jnp.concatenate([_dot(p[hd * tq:(hd + 1) * tq], v_heads[hd]) for hd in range(2)], axis=0)
        acc_ref[...] = jnp.exp2(m_prev - m_new) * acc_ref[...] + pv
        m_ref[...] = m_new

    def block_live(j):
        tail = pl.multiple_of(jnp.maximum(j, 0) * tk + tk - LANES, LANES)
        live = False
        for hd in range(2):
            top = qk_bound_ref[0, 0] + jnp.max(nf_ref[hd:hd + 1, pl.ds(tail, LANES)], axis=1, keepdims=True)
            live = jnp.logical_or(live, jnp.max(top - m_ref[hd * tq:(hd + 1) * tq, :]) > EXP2_ZERO_BELOW)
        return live

    sweep(diag_block, True)

    def cond(carry):
        j, live = carry
        return jnp.logical_and(j >= 0, live)

    def body(carry):
        j, _ = carry
        sweep(j, False)
        return j - 1, block_live(j - 1)

    lax.while_loop(cond, body, (diag_block - 1, block_live(diag_block - 1)))

    acc_a = acc_ref[0:tq, :]
    acc_b = acc_ref[tq:, :]
    den_a = jnp.broadcast_to(acc_a[:, den_lane[0]:den_lane[0] + 1], acc_a.shape)
    den_b = jnp.broadcast_to(acc_b[:, den_lane[1]:den_lane[1] + 1], acc_b.shape)
    o_ref[...] = jnp.where(first_q, acc_a / den_a, acc_b / den_b).astype(o_ref.dtype)


def _forgetting_attention(qk_bound, q, k, v, neg_f, batch, seq):
    m, width = q.shape
    tq = FORGET_Q_BLOCK
    assert FORGET_K_BLOCK % tq == 0 and seq % FORGET_K_BLOCK == 0
    nq = seq // tq
    n_pairs = width // LANES
    neg_f = neg_f.reshape(n_pairs, 2, m)
    qspec = pl.BlockSpec((tq, LANES), lambda b, p, i: (b * nq + i, p))
    kvspec = pl.BlockSpec((seq, LANES), lambda b, p, i: (b, p))
    return pl.pallas_call(
        _forget_kernel,
        grid=(batch, n_pairs, nq),
        in_specs=[pl.BlockSpec(memory_space=pltpu.SMEM), qspec, kvspec, kvspec,
                  pl.BlockSpec((None, 2, seq), lambda b, p, i: (p, 0, b))],
        out_specs=qspec,
        out_shape=jax.ShapeDtypeStruct((m, width), BF16),
        scratch_shapes=[pltpu.VMEM((2 * tq, LANES), F32),
                        pltpu.VMEM((2 * tq, LANES), F32)],
        compiler_params=_params("parallel", "parallel", "arbitrary"),
        name="forgetting_attention",
    )(qk_bound, q, k, v, neg_f)


def _pool_kernel(x_ref, w_ref, s_ref, o_ref, pad_ref, *, chunk):
    g = pl.program_id(1)
    seq = x_ref.shape[0]
    pad_ref[0:MAX_WINDOW, :] = jnp.zeros((MAX_WINDOW, LANES), F32)
    pad_ref[MAX_WINDOW:, :] = x_ref[...]
    w = w_ref[...].astype(BF16)
    for gi, win in enumerate(POOL_WINDOWS):
        @pl.when(g == gi)
        def _(win=win):
            for r0 in range(0, seq, chunk):
                x = pad_ref[MAX_WINDOW + r0:MAX_WINDOW + r0 + chunk, :]
                total = x
                for back in range(1, win):
                    total = total + pad_ref[MAX_WINDOW + r0 - back:MAX_WINDOW + r0 - back + chunk, :]
                pos = r0 + lax.broadcasted_iota(jnp.int32, (chunk, LANES), 0)
                count = jnp.minimum(pos + 1, win).astype(F32)
                pooled = total / count - x
                o_ref[r0:r0 + chunk, :] = (_dot(pooled.astype(BF16), w) * s_ref[...]).astype(o_ref.dtype)


def _multiscale_pool(xd, pool_w, pool_scale, batch, seq):
    m, width = xd.shape
    n_groups = width // LANES
    spec = pl.BlockSpec((seq, LANES), lambda b, g: (b, g))
    return pl.pallas_call(
        functools.partial(_pool_kernel, chunk=min(seq, 512)),
        grid=(batch, n_groups),
        in_specs=[spec,
                  pl.BlockSpec((None, LANES, LANES), lambda b, g: (g, 0, 0)),
                  pl.BlockSpec((1, LANES), lambda b, g: (0, g))],
        out_specs=spec,
        out_shape=jax.ShapeDtypeStruct((m, width), BF16),
        scratch_shapes=[pltpu.VMEM((seq + MAX_WINDOW, LANES), F32)],
        compiler_params=_params("parallel", "parallel"),
        name="multiscale_pool",
    )(xd, pool_w, pool_scale)


def _route(logits):
    lane = lax.broadcasted_iota(jnp.int32, logits.shape, 1)
    is_group = lane < N_GROUPS
    gl = jnp.where(is_group, logits, NEG_BIG)
    g_max = jnp.max(gl, axis=1, keepdims=True)
    g_idx = jnp.min(jnp.where(gl == g_max, lane, LANES), axis=1, keepdims=True)
    g_den = jnp.sum(jnp.where(is_group, jnp.exp(gl - g_max), 0.0), axis=1, keepdims=True)
    g_p = 1.0 / g_den
    lo = N_GROUPS + g_idx * EXPERTS_PER_GROUP
    in_group = jnp.logical_and(lane >= lo, lane < lo + EXPERTS_PER_GROUP)
    el = jnp.where(in_group, logits, NEG_BIG)
    top1 = jnp.max(el, axis=1, keepdims=True)
    idx1 = jnp.min(jnp.where(el == top1, lane, LANES), axis=1, keepdims=True)
    el2 = jnp.where(lane == idx1, NEG_BIG, el)
    top2 = jnp.max(el2, axis=1, keepdims=True)
    idx2 = jnp.min(jnp.where(el2 == top2, lane, LANES), axis=1, keepdims=True)
    e2 = jnp.exp(top2 - top1)
    den = 1.0 + e2
    w1 = (1.0 / den) * g_p
    w2 = (e2 / den) * g_p
    return jnp.where(lane == idx1, w1, 0.0) + jnp.where(lane == idx2, w2, 0.0), g_idx


def _moe_kernel(h_ref, ya_ref, yb_ref, wo_ref, g_ref, wr_ref, br_ref,
                wgate_ref, wup_ref, wd_ref, p_ref, wp_ref, gn_ref, wg_ref, o_ref,
                h1_ref, xs_ref, gs_ref, ys_ref, pos_ref, cnt_ref):
    t, d = h_ref.shape
    win = MOE_WINDOW

    def sort_by_group():
        half = ya_ref.shape[1]
        h1 = h_ref[...] + _dot(ya_ref[...], wo_ref[0:half, :]) + _dot(yb_ref[...], wo_ref[half:, :])
        h1_ref[...] = h1
        hn = _rms_normalize(h1) * g_ref[...]
        hn_hi, hn_lo = _split_bf16(hn)
        both = _dot(hn_hi, wr_ref[...])
        logits = both[:, :LANES] + (both[:, LANES:] + _dot(hn_lo, wr_ref[:, :LANES])) + br_ref[...]
        gates, g_idx = _route(logits)
        lane = lax.broadcasted_iota(jnp.int32, (t, LANES), 1)
        onehot = jnp.where(lane == g_idx, 1.0, 0.0)
        r_i = lax.broadcasted_iota(jnp.int32, (t, t), 0)
        c_i = lax.broadcasted_iota(jnp.int32, (t, t), 1)
        earlier = jnp.where(c_i < r_i, 1.0, 0.0).astype(BF16)
        rank = _dot(earlier, onehot.astype(BF16))
        counts = jnp.broadcast_to(rank[t - 1:t, :] + onehot[t - 1:t, :], (8, LANES))
        lr = lax.broadcasted_iota(jnp.int32, (LANES, LANES), 0)
        lc = lax.broadcasted_iota(jnp.int32, (LANES, LANES), 1)
        offsets = _dot_exact_rhs(counts, jnp.where(lr < lc, 1.0, 0.0).astype(BF16))
        pos = jnp.sum(onehot * (offsets[0:1, :] + rank), axis=1, keepdims=True)
        pos_b = jnp.broadcast_to(pos, (t, LANES)).astype(jnp.int32)
        pos_ref[...] = pos_b
        cnt_ref[0:8, :] = counts.astype(jnp.int32)
        cnt_ref[8:16, :] = offsets.astype(jnp.int32)
        pos_row = jnp.transpose(pos_b)[0:1, :]
        perm = jnp.where(r_i == pos_row, 1.0, 0.0).astype(BF16)
        g_hi, g_lo = _split_bf16(gates)
        srt = _dot(perm, jnp.concatenate([hn_hi, g_hi, g_lo], axis=1))
        xs_ref[0:t, :] = srt[:, :d].astype(BF16)
        xs_ref[t:, :] = jnp.zeros((win, d), BF16)
        gs_ref[0:t, :] = srt[:, d:d + LANES] + srt[:, d + LANES:]
        gs_ref[t:, :] = jnp.zeros((win, LANES), F32)
        ys_ref[...] = jnp.zeros_like(ys_ref)

    def run_group(grp, carry):
        lane1 = lax.broadcasted_iota(jnp.int32, (1, LANES), 1)
        count = jnp.sum(jnp.where(lane1 == grp, cnt_ref[0:1, :], 0))
        start = jnp.sum(jnp.where(lane1 == grp, cnt_ref[8:9, :], 0))
        base = (start // BF16_ROWS) * BF16_ROWS
        n_windows = (start - base + count + win - 1) // win

        def window(w, c):
            r0 = pl.multiple_of(base + w * win, BF16_ROWS)
            xs = xs_ref[pl.ds(r0, win), :]
            gs = gs_ref[pl.ds(r0, win), :]
            lane = lax.broadcasted_iota(jnp.int32, (win, LANES), 1)
            parts = []
            for e in range(EXPERTS_PER_GROUP):
                expert = grp * EXPERTS_PER_GROUP + e
                hg = _dot(xs, wgate_ref[expert])
                hu = _dot(xs, wup_ref[expert])
                gate = jnp.sum(jnp.where(lane == N_GROUPS + expert, gs, 0.0), axis=1, keepdims=True)
                parts.append(((hg * jax.nn.sigmoid(hg)) * hu * gate).astype(BF16))
            ys_ref[pl.ds(r0, win), :] += _dot(jnp.concatenate(parts, axis=1), wd_ref[grp])
            return c

        return lax.fori_loop(0, n_windows, window, carry)

    def unsort_and_embed():
        c_i = lax.broadcasted_iota(jnp.int32, (t, t), 1)
        pos_t = jnp.concatenate([pos_ref[...]] * (t // LANES), axis=1)
        unperm = jnp.where(c_i == pos_t, 1.0, 0.0).astype(BF16)
        h2 = h1_ref[...] + _dot(unperm, ys_ref[0:t, :].astype(BF16))
        emb = _rms_normalize(_dot(p_ref[...].astype(BF16), wp_ref[...])) * gn_ref[...]
        emb_gate = jax.nn.sigmoid(_dot(_rms_normalize(h2).astype(BF16), wg_ref[...]))
        o_ref[...] = h2 + emb_gate * emb

    sort_by_group()
    lax.fori_loop(0, N_GROUPS, run_group, 0)
    unsort_and_embed()


def _moe(layer, h, ya, yb, wo, g, wr, br, wgate, wup, wd, p, wp, gn, wg):
    m, d = h.shape
    n_tiles = m // MOE_TILE
    rows = lambda a: pl.BlockSpec((MOE_TILE, a.shape[1]), lambda i: (i, 0))
    full = lambda a: pl.BlockSpec(a.shape, lambda i: (0,) * a.ndim, pipeline_mode=pl.Buffered(1))
    of_layer = lambda a: pl.BlockSpec((None,) + a.shape[1:], lambda i: (layer,) + (0,) * (a.ndim - 1),
                                      pipeline_mode=pl.Buffered(1))
    p_rows = pl.BlockSpec((MOE_TILE, p.shape[1]), lambda i: (layer * n_tiles + i, 0))
    return pl.pallas_call(
        _moe_kernel,
        grid=(n_tiles,),
        in_specs=[rows(h), rows(ya), rows(yb), full(wo), full(g), full(wr), full(br),
                  of_layer(wgate), of_layer(wup), of_layer(wd), p_rows, of_layer(wp), full(gn),
                  of_layer(wg)],
        out_specs=rows(h),
        out_shape=jax.ShapeDtypeStruct((m, d), F32),
        scratch_shapes=[pltpu.VMEM((MOE_TILE, d), F32),
                        pltpu.VMEM((MOE_TILE + MOE_WINDOW, d), BF16),
                        pltpu.VMEM((MOE_TILE + MOE_WINDOW, LANES), F32),
                        pltpu.VMEM((MOE_TILE + MOE_WINDOW, d), F32),
                        pltpu.VMEM((MOE_TILE, LANES), jnp.int32),
                        pltpu.VMEM((16, LANES), jnp.int32)],
        compiler_params=pltpu.CompilerParams(dimension_semantics=("parallel",),
                                             vmem_limit_bytes=MOE_VMEM_LIMIT),
        name="outproj_moe_ple",
    )(h, ya, yb, wo, g, wr, br, wgate, wup, wd, p, wp, gn, wg)


def _row(a):
    return a.reshape(1, -1).astype(F32)


def _hi_lo(w):
    hi = w.astype(BF16)
    return hi, (w - hi.astype(F32)).astype(BF16)


def _even_mixer(h, g, w_in, sgu_w, sgu_b, sgu_norm, batch, seq):
    bias = jnp.repeat(sgu_b.T, HEAD_DIM, axis=1)
    ya, q, k, vb = _even_in_proj(h, _row(g), w_in.astype(BF16), sgu_w, bias, _row(sgu_norm))
    yb = _stick_breaking(q, k, vb, batch, seq)
    return ya, yb


def _odd_mixer(h, g, w_in, forget_bias, q_norm, k_norm, pool_w, pool_scale, w_out, batch, seq):
    n_heads = forget_bias.shape[0]
    width = n_heads * HEAD_DIM
    order = jnp.argsort(forget_bias)
    cols = (order[:, None] * HEAD_DIM + jnp.arange(HEAD_DIM)[None, :]).reshape(-1)
    w_main = jnp.concatenate([w_in[:, c * width:(c + 1) * width][:, cols] for c in range(3)]
                             + [w_in[:, 3 * width + n_heads:]], axis=1).astype(BF16)
    wf_hi, wf_lo = _hi_lo(w_in[:, 3 * width:3 * width + n_heads][:, order])
    wf = jnp.zeros((w_in.shape[0], LANES), BF16).at[:, :n_heads].set(wf_hi).at[:, n_heads:2 * n_heads].set(wf_lo)
    forget_bias = forget_bias[order]
    w_out = jnp.concatenate([w_out[:width][cols], w_out[width:]], axis=0)
    gain = lambda t: jnp.tile(t, LANES // HEAD_DIM).reshape(1, LANES).astype(F32)
    q, k, v, xd, neg_f = _odd_in_proj(h, _row(g), w_main, wf,
                                      forget_bias.reshape(n_heads, 1).astype(F32),
                                      gain(q_norm), gain(k_norm), seq)
    qk_bound = (1.02 * HEAD_DIM * HEAD_DIM ** -0.5 * LOG2_E
                * jnp.max(jnp.abs(q_norm)) * jnp.max(jnp.abs(k_norm))).reshape(1, 1).astype(F32)
    yc = _forgetting_attention(qk_bound, q, k, v, neg_f, batch, seq)
    yd = _multiscale_pool(xd, pool_w, _row(pool_scale), batch, seq)
    return yc, yd, w_out


def _moe_layer(layer, h, ya, yb, w_out, g, rg_w, rg_b, re_w, re_b, w_gate, w_up, w_down,
               p, ple_w_proj, ple_norm, ple_w_gate):
    d = h.shape[1]
    n_route = N_GROUPS + N_EXPERTS
    wr = jnp.zeros((d, LANES), F32).at[:, :N_GROUPS].set(rg_w).at[:, N_GROUPS:n_route].set(re_w)
    br = jnp.zeros((1, LANES), F32).at[0, :N_GROUPS].set(rg_b).at[0, N_GROUPS:n_route].set(re_b)
    wr = jnp.concatenate(_hi_lo(wr), axis=1)
    return _moe(layer, h, ya, yb, w_out.astype(BF16), _row(g), wr, br,
                w_gate, w_up, w_down, p, ple_w_proj, _row(ple_norm), ple_w_gate)


def kernel(x, p, norm_mix, norm_ffn, even_w_in, sgu_w, sgu_b, sgu_norm, even_w_out, odd_w_in, forget_bias, q_norm, k_norm, pool_w, pool_scale, odd_w_out, router_group_w, router_group_b, router_expert_w, router_expert_b, expert_w_gate, expert_w_up, expert_w_down, ple_w_proj, ple_norm, ple_w_gate):
    batch, seq, d = x.shape
    depth = p.shape[0]
    h = x.reshape(batch * seq, d)
    w_gate_all = expert_w_gate.astype(BF16)
    w_up_all = expert_w_up.astype(BF16)
    w_down_all = expert_w_down.reshape(depth, N_GROUPS, EXPERTS_PER_GROUP * D_EXPERT, d).astype(BF16)
    ple_w_proj_all = ple_w_proj.astype(BF16)
    ple_w_gate_all = ple_w_gate.astype(BF16)
    p_all = p.reshape(depth * batch * seq, -1)
    for i in range(depth):
        j = i // 2
        if i % 2 == 0:
            ya, yb = _even_mixer(h, norm_mix[i], even_w_in[j], sgu_w[j], sgu_b[j], sgu_norm[j],
                                 batch, seq)
            w_out = even_w_out[j]
        else:
            ya, yb, w_out = _odd_mixer(h, norm_mix[i], odd_w_in[j], forget_bias[j], q_norm[j],
                                       k_norm[j], pool_w[j], pool_scale[j], odd_w_out[j], batch, seq)
        h = _moe_layer(i, h, ya, yb, w_out, norm_ffn[i], router_group_w[i], router_group_b[i],
                       router_expert_w[i], router_expert_b[i], w_gate_all, w_up_all, w_down_all,
                       p_all, ple_w_proj_all, ple_norm[i], ple_w_gate_all)
    return h.reshape(batch, seq, d)
```

```python
import functools

import jax
import jax.numpy as jnp
from jax import lax
from jax.experimental import pallas as pl
from jax.experimental.pallas import tpu as pltpu

F32 = jnp.float32
BF16 = jnp.bfloat16

HEAD_DIM = 64
LANES = 128
CHUNK = 128
POOL_WINDOWS = (2, 4, 8, 16)
MAX_WINDOW = max(POOL_WINDOWS)
N_GROUPS = 4
EXPERTS_PER_GROUP = 4
N_EXPERTS = N_GROUPS * EXPERTS_PER_GROUP
D_EXPERT = 256
RMS_EPS = 1e-6
ATT_BLOCK = 256
FORGET_Q_BLOCK = 512
FORGET_K_BLOCK = 512
LOG2_E = 1.4426950408889634
ROW_TILE = 512
MOE_TILE = 512
MOE_WINDOW = 176
MOE_SMALL_WINDOW = 112
BF16_ROWS = 16
VMEM_LIMIT = 48 * 1024 * 1024
MOE_VMEM_LIMIT = 56 * 1024 * 1024
EXP2_ZERO_BELOW = -150.0
NEG_BIG = -1e30


def _params(*sem):
    return pltpu.CompilerParams(dimension_semantics=sem, vmem_limit_bytes=VMEM_LIMIT)


def _split_bf16(x):
    hi = x.astype(BF16)
    lo = (x - hi.astype(F32)).astype(BF16)
    return hi, lo


def _dot(a, b):
    return jnp.dot(a, b, preferred_element_type=F32)


def _dot_nt(a, b):
    return lax.dot_general(a, b, (((1,), (1,)), ((), ())), preferred_element_type=F32)


def _dot_exact_rhs(x, m):
    hi, lo = _split_bf16(x)
    return _dot(hi, m) + _dot(lo, m)


def _rms_normalize(x):
    return x * lax.rsqrt(jnp.mean(x * x, axis=-1, keepdims=True) + RMS_EPS)


def _log_sigmoid(x):
    return jnp.minimum(x, 0.0) - jnp.log1p(jnp.exp(-jnp.abs(x)))


def _gelu_tanh(x):
    return 0.5 * x * (1.0 + jnp.tanh(0.7978845608028654 * (x + 0.044715 * (x * x * x))))


def _head_group_matrix():
    r = lax.broadcasted_iota(jnp.int32, (LANES, LANES), 0) // HEAD_DIM
    c = lax.broadcasted_iota(jnp.int32, (LANES, LANES), 1) // HEAD_DIM
    return jnp.where(r == c, 1.0, 0.0).astype(BF16)


def _first_head_mask(shape):
    return lax.broadcasted_iota(jnp.int32, shape, len(shape) - 1) < HEAD_DIM


def _even_in_kernel(h_ref, g_ref, w_ref, ws_ref, bs_ref, gv_ref, ya_ref, q_ref, k_ref, vb_ref, *, width):
    hn = (_rms_normalize(h_ref[...]) * g_ref[...]).astype(BF16)
    proj = lambda c: _dot(hn, w_ref[:, c * width:(c + 1) * width])
    n_pairs = width // LANES
    n_chunks = h_ref.shape[0] // CHUNK
    gmat = _head_group_matrix()
    v_all = _gelu_tanh(proj(1))
    u_all = _gelu_tanh(proj(0))
    q_ref[...] = (proj(2) * (HEAD_DIM ** -0.5 * LOG2_E)).astype(q_ref.dtype)
    ms_all = [_dot((v_all[:, p * LANES:(p + 1) * LANES] ** 2).astype(BF16), gmat) for p in range(n_pairs)]
    k_ref[...] = proj(3).astype(k_ref.dtype)
    vb_ref[...] = proj(4).astype(vb_ref.dtype)

    first = _first_head_mask((CHUNK, LANES))
    tril = (lax.broadcasted_iota(jnp.int32, (CHUNK, CHUNK), 0)
            >= lax.broadcasted_iota(jnp.int32, (CHUNK, CHUNK), 1))
    for p in range(n_pairs):
        cols = slice(p * LANES, (p + 1) * LANES)
        w_a = jnp.where(tril, ws_ref[2 * p], 0.0).astype(BF16)
        w_b = jnp.where(tril, ws_ref[2 * p + 1], 0.0).astype(BF16)
        v = v_all[:, cols]
        ms = ms_all[p] * (1.0 / HEAD_DIM)
        vn = v * lax.rsqrt(ms + RMS_EPS) * gv_ref[:, cols]
        for c in range(n_chunks):
            rows = slice(c * CHUNK, (c + 1) * CHUNK)
            vc = vn[rows]
            mixed = (_dot(w_a, jnp.where(first, vc, 0.0).astype(BF16))
                     + _dot(w_b, jnp.where(first, 0.0, vc).astype(BF16)) + bs_ref[:, cols])
            ya_ref[rows, cols] = (u_all[rows, cols] * mixed).astype(ya_ref.dtype)


def _even_in_proj(h, g, w, w_s, bias, g_v):
    m, d = h.shape
    width = w.shape[1] // 5
    out = jax.ShapeDtypeStruct((m, width), BF16)
    row = pl.BlockSpec((ROW_TILE, width), lambda i: (i, 0))
    full = lambda a: pl.BlockSpec(a.shape, lambda i: (0,) * a.ndim)
    return pl.pallas_call(
        functools.partial(_even_in_kernel, width=width),
        grid=(m // ROW_TILE,),
        in_specs=[pl.BlockSpec((ROW_TILE, d), lambda i: (i, 0)),
                  full(g), full(w), full(w_s), full(bias), full(g_v)],
        out_specs=[row] * 4,
        out_shape=[out] * 4,
        compiler_params=_params("parallel"),
        name="even_in_proj",
    )(h, g, w, w_s, bias, g_v)


def _stick_kernel(q_ref, k_ref, v_ref, o_ref, acc_ref, r_ref):
    blk = ATT_BLOCK
    n_pairs = q_ref.shape[1] // LANES
    i = pl.program_id(1)
    first_q = _first_head_mask((blk, LANES))
    row = lax.broadcasted_iota(jnp.int32, (blk, blk), 0)
    col = lax.broadcasted_iota(jnp.int32, (blk, blk), 1)
    suffix = jnp.where(row >= col, 1.0, 0.0).astype(BF16)
    suffix2 = jnp.concatenate([suffix, suffix], axis=0)
    strict = col < row
    acc_ref[...] = jnp.zeros_like(acc_ref)
    r_ref[...] = jnp.zeros_like(r_ref)

    def sweep(blocks):
        n_heads = 2 * n_pairs
        strict_all = jnp.concatenate([strict] * n_heads, axis=0)
        starts = [pl.multiple_of(j * blk, blk) for j, _ in blocks]
        q2s = []
        for p in range(n_pairs):
            q = q_ref[:, p * LANES:(p + 1) * LANES].astype(F32)
            q2s.append(jnp.concatenate([jnp.where(first_q, q, 0.0),
                                        jnp.where(first_q, 0.0, q)], axis=0).astype(BF16))
        zs = [jnp.concatenate([_dot_nt(q2s[p], k_ref[pl.ds(start, blk), p * LANES:(p + 1) * LANES])
                               for p in range(n_pairs)], axis=0) for start in starts]
        tails = []
        for z, (_, diagonal) in zip(zs, blocks):
            nz = -z
            log_1m_beta = jnp.minimum(nz, 0.0) - jnp.log2(1.0 + jnp.exp2(jnp.minimum(z, nz)))
            if diagonal:
                log_1m_beta = jnp.where(strict_all, log_1m_beta, 0.0)
            hi, lo = _split_bf16(log_1m_beta)
            tails.append(_dot(jnp.concatenate([hi, lo], axis=1), suffix2))
        r = r_ref[...]
        for z, tail, start, (_, diagonal) in zip(zs, tails, starts, blocks):
            weights = jnp.exp2(z + tail + jnp.concatenate([r] * (blk // LANES), axis=1))
            if diagonal:
                weights = jnp.where(strict_all, weights, 0.0)
            weights = weights.astype(BF16)
            r = r + jnp.broadcast_to(tail[:, 0:1], r.shape)
            for p in range(n_pairs):
                cols = slice(p * LANES, (p + 1) * LANES)
                v = v_ref[pl.ds(start, blk), cols].astype(F32)
                v2 = jnp.concatenate([jnp.where(first_q, v, 0.0), jnp.where(first_q, 0.0, v)], axis=0)
                w2 = jnp.concatenate([weights[2 * p * blk:(2 * p + 1) * blk],
                                      weights[(2 * p + 1) * blk:(2 * p + 2) * blk]], axis=1)
                acc_ref[p] += _dot(w2, v2.astype(BF16))
        r_ref[...] = r

    @pl.when(i == 0)
    def _():
        sweep([(i, True)])

    @pl.when(i > 0)
    def _():
        sweep([(i, True), (i - 1, False)])

    def cond(carry):
        j, live = carry
        return jnp.logical_and(j >= 0, live)

    def any_live():
        return jnp.max(r_ref[...]) > EXP2_ZERO_BELOW

    def body(carry):
        j, _ = carry
        sweep([(j, False)])
        return j - 1, any_live()

    lax.while_loop(cond, body, (i - 2, any_live()))
    for p in range(n_pairs):
        o_ref[:, p * LANES:(p + 1) * LANES] = acc_ref[p].astype(o_ref.dtype)


def _stick_breaking(q, k, v, batch, seq):
    m, width = q.shape
    nq = seq // ATT_BLOCK
    n_pairs = width // LANES
    qspec = pl.BlockSpec((ATT_BLOCK, width), lambda b, i: (b * nq + i, 0))
    kvspec = pl.BlockSpec((seq, width), lambda b, i: (b, 0))
    return pl.pallas_call(
        _stick_kernel,
        grid=(batch, nq),
        in_specs=[qspec, kvspec, kvspec],
        out_specs=qspec,
        out_shape=jax.ShapeDtypeStruct((m, width), BF16),
        scratch_shapes=[pltpu.VMEM((n_pairs, ATT_BLOCK, LANES), F32),
                        pltpu.VMEM((2 * n_pairs * ATT_BLOCK, LANES), F32)],
        compiler_params=_params("parallel", "arbitrary"),
        name="stick_breaking",
    )(q, k, v)


def _odd_in_kernel(h_ref, g_ref, w_ref, wf_ref, fb_ref, qn_ref, kn_ref, pw_ref, ps_ref,
                   q_ref, k_ref, v_ref, yd_ref, nf_ref, carry_ref, pad_ref, *, width, tiles_per_seq):
    i = pl.program_id(0)
    tile = h_ref.shape[0]

    @pl.when(i % tiles_per_seq == 0)
    def _():
        carry_ref[...] = jnp.zeros_like(carry_ref)
        pad_ref[0:MAX_WINDOW, :] = jnp.zeros((MAX_WINDOW, width), F32)

    hn32 = _rms_normalize(h_ref[...]) * g_ref[...]
    hn, hn_lo = _split_bf16(hn32)
    gmat = _head_group_matrix()

    def head_norm(x, gain_ref):
        parts = []
        for p in range(width // LANES):
            xp = x[:, p * LANES:(p + 1) * LANES]
            ms = _dot((xp * xp).astype(BF16), gmat) * (1.0 / HEAD_DIM)
            parts.append(xp * lax.rsqrt(ms + RMS_EPS) * gain_ref[...])
        return jnp.concatenate(parts, axis=1)

    n_heads = fb_ref.shape[0]
    f_t = jnp.transpose(_dot(hn, wf_ref[...]) + _dot(hn_lo, wf_ref[...]))
    f_logit = f_t[0:n_heads] + f_t[n_heads:2 * n_heads]

    pad_ref[MAX_WINDOW:, :] = _dot(hn, w_ref[:, 3 * width:4 * width])
    pos = (i % tiles_per_seq) * tile + lax.broadcasted_iota(jnp.int32, (tile, LANES), 0)
    pooled = []
    for gi, win in enumerate(POOL_WINDOWS):
        cols = slice(gi * LANES, (gi + 1) * LANES)
        x = pad_ref[MAX_WINDOW:, cols]
        total = x
        for back in range(1, win):
            total = total + pad_ref[MAX_WINDOW - back:MAX_WINDOW - back + tile, cols]
        pooled.append((total / jnp.minimum(pos + 1, win).astype(F32) - x).astype(BF16))
    pad_ref[0:MAX_WINDOW, :] = pad_ref[tile:tile + MAX_WINDOW, :]

    qf = head_norm(_dot(hn, w_ref[:, 0:width]), qn_ref) * (HEAD_DIM ** -0.5 * LOG2_E)
    q_ref[...] = qf.astype(q_ref.dtype)

    log_f = _log_sigmoid(f_logit + fb_ref[...])
    n_heads, tile = log_f.shape
    r = lax.broadcasted_iota(jnp.int32, (LANES, LANES), 0)
    c = lax.broadcasted_iota(jnp.int32, (LANES, LANES), 1)
    prefix = jnp.where(r <= c, 1.0, 0.0).astype(BF16)
    local = []
    for s in range(tile // LANES):
        x = log_f[:, s * LANES:(s + 1) * LANES]
        x1 = x.astype(BF16)
        rem = x - x1.astype(F32)
        x2 = rem.astype(BF16)
        x3 = (rem - x2.astype(F32)).astype(BF16)
        local.append(_dot(x1, prefix) + (_dot(x2, prefix) + _dot(x3, prefix)))

    k_ref[...] = head_norm(_dot(hn, w_ref[:, width:2 * width]), kn_ref).astype(k_ref.dtype)

    carry = carry_ref[...]
    for s in range(tile // LANES):
        cum = local[s] + carry
        nf_ref[:, s * LANES:(s + 1) * LANES] = cum * (-LOG2_E)
        carry = jnp.broadcast_to(cum[:, LANES - 1:LANES], carry.shape)
    carry_ref[...] = carry

    v_ref[...] = _dot(hn, w_ref[:, 2 * width:3 * width]).astype(v_ref.dtype)

    for gi in range(len(POOL_WINDOWS)):
        cols = slice(gi * LANES, (gi + 1) * LANES)
        mixed = _dot(pooled[gi], pw_ref[gi].astype(BF16)) * ps_ref[:, cols]
        yd_ref[:, cols] = mixed.astype(yd_ref.dtype)


def _odd_in_proj(h, g, w, wf, fb, qn, kn, pool_w, pool_scale, seq):
    m, d = h.shape
    width = w.shape[1] // 4
    n_heads = fb.shape[0]
    row = pl.BlockSpec((ROW_TILE, width), lambda i: (i, 0))
    full = lambda a: pl.BlockSpec(a.shape, lambda i: (0,) * a.ndim)
    return pl.pallas_call(
        functools.partial(_odd_in_kernel, width=width, tiles_per_seq=seq // ROW_TILE),
        grid=(m // ROW_TILE,),
        in_specs=[pl.BlockSpec((ROW_TILE, d), lambda i: (i, 0)),
                  full(g), full(w), full(wf), full(fb), full(qn), full(kn), full(pool_w),
                  full(pool_scale)],
        out_specs=[row, row, row, row, pl.BlockSpec((n_heads, ROW_TILE), lambda i: (0, i))],
        out_shape=[jax.ShapeDtypeStruct((m, width), BF16)] * 4
                  + [jax.ShapeDtypeStruct((n_heads, m), F32)],
        scratch_shapes=[pltpu.VMEM((n_heads, LANES), F32),
                        pltpu.VMEM((MAX_WINDOW + ROW_TILE, width), F32)],
        compiler_params=_params("arbitrary"),
        name="odd_in_proj",
    )(h, g, w, wf, fb, qn, kn, pool_w, pool_scale)


def _forget_kernel(qk_bound_ref, q_ref, k_ref, v_ref, nf_ref, o_ref, acc_ref, m_ref):
    tq, tk = FORGET_Q_BLOCK, FORGET_K_BLOCK
    i = pl.program_id(2)
    first_q = _first_head_mask((tq, LANES))
    lane = lax.broadcasted_iota(jnp.int32, (tk, LANES), 1)
    first_k = lane < HEAD_DIM
    q = q_ref[...].astype(F32)
    q2 = jnp.concatenate([jnp.where(first_q, q, 0.0), jnp.where(first_q, 0.0, q)], axis=0).astype(BF16)
    den_lane = (HEAD_DIM, 0)
    den_cols = [jnp.where(lane == dl, 1.0, 0.0) for dl in den_lane]
    acc_ref[...] = jnp.zeros_like(acc_ref)
    m_ref[...] = jnp.full_like(m_ref, NEG_BIG)
    diag_block = (i * tq) // tk

    def sweep(j, diagonal):
        start = pl.multiple_of(j * tk, tk)
        k = k_ref[pl.ds(start, tk), :]
        v = v_ref[pl.ds(start, tk), :].astype(F32)
        v_heads = (jnp.where(first_k, v, den_cols[0]).astype(BF16),
                   jnp.where(first_k, den_cols[1], v).astype(BF16))
        if diagonal:
            q_pos = i * tq + lax.broadcasted_iota(jnp.int32, (tq, tk), 0)
            k_pos = start + lax.broadcasted_iota(jnp.int32, (tq, tk), 1)
            causal = k_pos <= q_pos
        s2 = _dot_nt(q2, k)
        s_heads = [s2[hd * tq:(hd + 1) * tq] + nf_ref[hd:hd + 1, pl.ds(start, tk)] for hd in range(2)]
        if diagonal:
            s_heads = [jnp.where(causal, s, NEG_BIG) for s in s_heads]
        s = jnp.concatenate(s_heads, axis=0)
        m_prev = m_ref[...]
        m_new = jnp.maximum(m_prev, jnp.broadcast_to(jnp.max(s, axis=1, keepdims=True), m_prev.shape))
        p = jnp.exp2(s - jnp.concatenate([m_new] * (tk // LANES), axis=1)).astype(BF16)
        pv = jnp.concatenate([_dot(p[hd * tq:(hd + 1) * tq], v_heads[hd]) for hd in range(2)], axis=0)
        acc_ref[...] = jnp.exp2(m_prev - m_new) * acc_ref[...] + pv
        m_ref[...] = m_new

    def block_live(j):
        tail = pl.multiple_of(jnp.maximum(j, 0) * tk + tk - LANES, LANES)
        live = False
        for hd in range(2):
            top = qk_bound_ref[0, 0] + jnp.max(nf_ref[hd:hd + 1, pl.ds(tail, LANES)], axis=1, keepdims=True)
            live = jnp.logical_or(live, jnp.max(top - m_ref[hd * tq:(hd + 1) * tq, :]) > EXP2_ZERO_BELOW)
        return live

    sweep(diag_block, True)

    def cond(carry):
        j, live = carry
        return jnp.logical_and(j >= 0, live)

    def body(carry):
        j, _ = carry
        sweep(j, False)
        return j - 1, block_live(j - 1)

    lax.while_loop(cond, body, (diag_block - 1, block_live(diag_block - 1)))

    acc_a = acc_ref[0:tq, :]
    acc_b = acc_ref[tq:, :]
    den_a = jnp.broadcast_to(acc_a[:, den_lane[0]:den_lane[0] + 1], acc_a.shape)
    den_b = jnp.broadcast_to(acc_b[:, den_lane[1]:den_lane[1] + 1], acc_b.shape)
    o_ref[...] = jnp.where(first_q, acc_a / den_a, acc_b / den_b).astype(o_ref.dtype)


def _forgetting_attention(qk_bound, q, k, v, neg_f, batch, seq):
    m, width = q.shape
    tq = FORGET_Q_BLOCK
    assert FORGET_K_BLOCK % tq == 0 and seq % FORGET_K_BLOCK == 0
    nq = seq // tq
    n_pairs = width // LANES
    neg_f = neg_f.reshape(n_pairs, 2, m)
    qspec = pl.BlockSpec((tq, LANES), lambda b, p, i: (b * nq + i, p))
    kvspec = pl.BlockSpec((seq, LANES), lambda b, p, i: (b, p))
    return pl.pallas_call(
        _forget_kernel,
        grid=(batch, n_pairs, nq),
        in_specs=[pl.BlockSpec(memory_space=pltpu.SMEM), qspec, kvspec, kvspec,
                  pl.BlockSpec((None, 2, seq), lambda b, p, i: (p, 0, b))],
        out_specs=qspec,
        out_shape=jax.ShapeDtypeStruct((m, width), BF16),
        scratch_shapes=[pltpu.VMEM((2 * tq, LANES), F32),
                        pltpu.VMEM((2 * tq, LANES), F32)],
        compiler_params=_params("parallel", "parallel", "arbitrary"),
        name="forgetting_attention",
    )(qk_bound, q, k, v, neg_f)


def _route(logits):
    lane = lax.broadcasted_iota(jnp.int32, logits.shape, 1)
    is_group = lane < N_GROUPS
    gl = jnp.where(is_group, logits, NEG_BIG)
    g_max = jnp.max(gl, axis=1, keepdims=True)
    g_idx = jnp.min(jnp.where(gl == g_max, lane, LANES), axis=1, keepdims=True)
    g_den = jnp.sum(jnp.where(is_group, jnp.exp(gl - g_max), 0.0), axis=1, keepdims=True)
    g_p = 1.0 / g_den
    lo = N_GROUPS + g_idx * EXPERTS_PER_GROUP
    in_group = jnp.logical_and(lane >= lo, lane < lo + EXPERTS_PER_GROUP)
    el = jnp.where(in_group, logits, NEG_BIG)
    top1 = jnp.max(el, axis=1, keepdims=True)
    idx1 = jnp.min(jnp.where(el == top1, lane, LANES), axis=1, keepdims=True)
    el2 = jnp.where(lane == idx1, NEG_BIG, el)
    top2 = jnp.max(el2, axis=1, keepdims=True)
    idx2 = jnp.min(jnp.where(el2 == top2, lane, LANES), axis=1, keepdims=True)
    e2 = jnp.exp(top2 - top1)
    den = 1.0 + e2
    w1 = (1.0 / den) * g_p
    w2 = (e2 / den) * g_p
    return jnp.where(lane == idx1, w1, 0.0) + jnp.where(lane == idx2, w2, 0.0), g_idx


def _moe_kernel(h_ref, ya_ref, yb_ref, wo_ref, g_ref, wr_ref, br_ref,
                wgate_ref, wup_ref, wd_ref, p_ref, wp_ref, gn_ref, wg_ref, o_ref,
                h1_ref, emb_ref, xs_ref, gs_ref, ys_ref, pos_ref, cnt_ref):
    t, d = h_ref.shape
    win = MOE_WINDOW

    def sort_by_group():
        half = ya_ref.shape[1]
        h1 = h_ref[...] + _dot(ya_ref[...], wo_ref[0:half, :]) + _dot(yb_ref[...], wo_ref[half:, :])
        h1_ref[...] = h1
        hn = _rms_normalize(h1) * g_ref[...]
        hn_hi, hn_lo = _split_bf16(hn)
        both = _dot(hn_hi, wr_ref[...])
        logits = both[:, :LANES] + (both[:, LANES:] + _dot(hn_lo, wr_ref[:, :LANES])) + br_ref[...]
        emb_ref[...] = _rms_normalize(_dot(p_ref[...].astype(BF16), wp_ref[...])) * gn_ref[...]
        gates, g_idx = _route(logits)
        lane = lax.broadcasted_iota(jnp.int32, (t, LANES), 1)
        onehot = jnp.where(lane == g_idx, 1.0, 0.0)
        r_i = lax.broadcasted_iota(jnp.int32, (t, t), 0)
        c_i = lax.broadcasted_iota(jnp.int32, (t, t), 1)
        earlier = jnp.where(c_i < r_i, 1.0, 0.0).astype(BF16)
        rank = _dot(earlier, onehot.astype(BF16))
        counts = jnp.broadcast_to(rank[t - 1:t, :] + onehot[t - 1:t, :], (8, LANES))
        lr = lax.broadcasted_iota(jnp.int32, (LANES, LANES), 0)
        lc = lax.broadcasted_iota(jnp.int32, (LANES, LANES), 1)
        offsets = _dot_exact_rhs(counts, jnp.where(lr < lc, 1.0, 0.0).astype(BF16))
        pos = jnp.sum(onehot * (offsets[0:1, :] + rank), axis=1, keepdims=True)
        pos_b = jnp.broadcast_to(pos, (t, LANES)).astype(jnp.int32)
        pos_ref[...] = pos_b
        cnt_ref[0:8, :] = counts.astype(jnp.int32)
        cnt_ref[8:16, :] = offsets.astype(jnp.int32)
        pos_row = jnp.transpose(pos_b)[0:1, :]
        perm = jnp.where(r_i == pos_row, 1.0, 0.0).astype(BF16)
        g_hi, g_lo = _split_bf16(gates)
        srt = _dot(perm, jnp.concatenate([hn_hi, g_hi, g_lo], axis=1))
        xs_ref[0:t, :] = srt[:, :d].astype(BF16)
        xs_ref[t:, :] = jnp.zeros((win, d), BF16)
        gs_ref[0:t, :] = srt[:, d:d + LANES] + srt[:, d + LANES:]
        gs_ref[t:, :] = jnp.zeros((win, LANES), F32)
        ys_ref[...] = jnp.zeros_like(ys_ref)

    def run_group(grp, carry):
        lane1 = lax.broadcasted_iota(jnp.int32, (1, LANES), 1)
        count = jnp.sum(jnp.where(lane1 == grp, cnt_ref[0:1, :], 0))
        start = jnp.sum(jnp.where(lane1 == grp, cnt_ref[8:9, :], 0))
        base = (start // BF16_ROWS) * BF16_ROWS
        rows = start - base + count
        n_full = rows // win
        rest = rows - n_full * win
        n_small = jnp.logical_and(rest > 0, rest <= MOE_SMALL_WINDOW).astype(jnp.int32)
        n_full = n_full + (rest > MOE_SMALL_WINDOW).astype(jnp.int32)

        def window(r0, size):
            r0 = pl.multiple_of(r0, BF16_ROWS)
            xs = xs_ref[pl.ds(r0, size), :]
            gs = gs_ref[pl.ds(r0, size), :]
            lane = lax.broadcasted_iota(jnp.int32, (size, LANES), 1)
            parts = []
            for e in range(EXPERTS_PER_GROUP):
                expert = grp * EXPERTS_PER_GROUP + e
                hg = _dot(xs, wgate_ref[expert])
                hu = _dot(xs, wup_ref[expert])
                gate = jnp.sum(jnp.where(lane == N_GROUPS + expert, gs, 0.0), axis=1, keepdims=True)
                parts.append(((hg * jax.nn.sigmoid(hg)) * hu * gate).astype(BF16))
            ys_ref[pl.ds(r0, size), :] += _dot(jnp.concatenate(parts, axis=1), wd_ref[grp])

        def full_window(w, c):
            window(base + w * win, win)
            return c

        def small_window(w, c):
            window(base + n_full * win, MOE_SMALL_WINDOW)
            return c

        carry = lax.fori_loop(0, n_full, full_window, carry)
        return lax.fori_loop(0, n_small, small_window, carry)

    def unsort_and_embed():
        c_i = lax.broadcasted_iota(jnp.int32, (t, t), 1)
        pos_t = jnp.concatenate([pos_ref[...]] * (t // LANES), axis=1)
        unperm = jnp.where(c_i == pos_t, 1.0, 0.0).astype(BF16)
        h2 = h1_ref[...] + _dot(unperm, ys_ref[0:t, :].astype(BF16))
        emb_gate = jax.nn.sigmoid(_dot(_rms_normalize(h2).astype(BF16), wg_ref[...]))
        o_ref[...] = h2 + emb_gate * emb_ref[...]

    sort_by_group()
    lax.fori_loop(0, N_GROUPS, run_group, 0)
    unsort_and_embed()


def _moe(layer, h, ya, yb, wo, g, wr, br, wgate, wup, wd, p, wp, gn, wg):
    m, d = h.shape
    n_tiles = m // MOE_TILE
    rows = lambda a: pl.BlockSpec((MOE_TILE, a.shape[1]), lambda i: (i, 0))
    full = lambda a: pl.BlockSpec(a.shape, lambda i: (0,) * a.ndim, pipeline_mode=pl.Buffered(1))
    of_layer = lambda a: pl.BlockSpec((None,) + a.shape[1:], lambda i: (layer,) + (0,) * (a.ndim - 1),
                                      pipeline_mode=pl.Buffered(1))
    p_rows = pl.BlockSpec((MOE_TILE, p.shape[1]), lambda i: (layer * n_tiles + i, 0))
    return pl.pallas_call(
        _moe_kernel,
        grid=(n_tiles,),
        in_specs=[rows(h), rows(ya), rows(yb), full(wo), full(g), full(wr), full(br),
                  of_layer(wgate), of_layer(wup), of_layer(wd), p_rows, of_layer(wp), full(gn),
                  of_layer(wg)],
        out_specs=rows(h),
        out_shape=jax.ShapeDtypeStruct((m, d), F32),
        scratch_shapes=[pltpu.VMEM((MOE_TILE, d), F32),
                        pltpu.VMEM((MOE_TILE, d), F32),
                        pltpu.VMEM((MOE_TILE + MOE_WINDOW, d), BF16),
                        pltpu.VMEM((MOE_TILE + MOE_WINDOW, LANES), F32),
                        pltpu.VMEM((MOE_TILE + MOE_WINDOW, d), F32),
                        pltpu.VMEM((MOE_TILE, LANES), jnp.int32),
                        pltpu.VMEM((16, LANES), jnp.int32)],
        compiler_params=pltpu.CompilerParams(dimension_semantics=("parallel",),
                                             vmem_limit_bytes=MOE_VMEM_LIMIT),
        name="outproj_moe_ple",
    )(h, ya, yb, wo, g, wr, br, wgate, wup, wd, p, wp, gn, wg)


def _row(a):
    return a.reshape(1, -1).astype(F32)


def _hi_lo(w):
    hi = w.astype(BF16)
    return hi, (w - hi.astype(F32)).astype(BF16)


def _even_mixer(h, g, w_in, sgu_w, sgu_b, sgu_norm, batch, seq):
    bias = jnp.repeat(sgu_b.T, HEAD_DIM, axis=1)
    ya, q, k, vb = _even_in_proj(h, _row(g), w_in.astype(BF16), sgu_w, bias, _row(sgu_norm))
    yb = _stick_breaking(q, k, vb, batch, seq)
    return ya, yb


def _odd_mixer(h, g, w_in, forget_bias, q_norm, k_norm, pool_w, pool_scale, w_out, batch, seq):
    n_heads = forget_bias.shape[0]
    width = n_heads * HEAD_DIM
    order = jnp.argsort(forget_bias)
    cols = (order[:, None] * HEAD_DIM + jnp.arange(HEAD_DIM)[None, :]).reshape(-1)
    w_main = jnp.concatenate([w_in[:, c * width:(c + 1) * width][:, cols] for c in range(3)]
                             + [w_in[:, 3 * width + n_heads:]], axis=1).astype(BF16)
    wf_hi, wf_lo = _hi_lo(w_in[:, 3 * width:3 * width + n_heads][:, order])
    wf = jnp.zeros((w_in.shape[0], LANES), BF16).at[:, :n_heads].set(wf_hi).at[:, n_heads:2 * n_heads].set(wf_lo)
    forget_bias = forget_bias[order]
    w_out = jnp.concatenate([w_out[:width][cols], w_out[width:]], axis=0)
    gain = lambda t: jnp.tile(t, LANES // HEAD_DIM).reshape(1, LANES).astype(F32)
    q, k, v, yd, neg_f = _odd_in_proj(h, _row(g), w_main, wf,
                                      forget_bias.reshape(n_heads, 1).astype(F32),
                                      gain(q_norm), gain(k_norm), pool_w, _row(pool_scale), seq)
    qk_bound = (1.02 * HEAD_DIM * HEAD_DIM ** -0.5 * LOG2_E
                * jnp.max(jnp.abs(q_norm)) * jnp.max(jnp.abs(k_norm))).reshape(1, 1).astype(F32)
    yc = _forgetting_attention(qk_bound, q, k, v, neg_f, batch, seq)
    return yc, yd, w_out


def _moe_layer(layer, h, ya, yb, w_out, g, rg_w, rg_b, re_w, re_b, w_gate, w_up, w_down,
               p, ple_w_proj, ple_norm, ple_w_gate):
    d = h.shape[1]
    n_route = N_GROUPS + N_EXPERTS
    wr = jnp.zeros((d, LANES), F32).at[:, :N_GROUPS].set(rg_w).at[:, N_GROUPS:n_route].set(re_w)
    br = jnp.zeros((1, LANES), F32).at[0, :N_GROUPS].set(rg_b).at[0, N_GROUPS:n_route].set(re_b)
    wr = jnp.concatenate(_hi_lo(wr), axis=1)
    return _moe(layer, h, ya, yb, w_out.astype(BF16), _row(g), wr, br,
                w_gate, w_up, w_down, p, ple_w_proj, _row(ple_norm), ple_w_gate)


def kernel(x, p, norm_mix, norm_ffn, even_w_in, sgu_w, sgu_b, sgu_norm, even_w_out, odd_w_in, forget_bias, q_norm, k_norm, pool_w, pool_scale, odd_w_out, router_group_w, router_group_b, router_expert_w, router_expert_b, expert_w_gate, expert_w_up, expert_w_down, ple_w_proj, ple_norm, ple_w_gate):
    batch, seq, d = x.shape
    depth = p.shape[0]
    h = x.reshape(batch * seq, d)
    w_gate_all = expert_w_gate.astype(BF16)
    w_up_all = expert_w_up.astype(BF16)
    w_down_all = expert_w_down.reshape(depth, N_GROUPS, EXPERTS_PER_GROUP * D_EXPERT, d).astype(BF16)
    ple_w_proj_all = ple_w_proj.astype(BF16)
    ple_w_gate_all = ple_w_gate.astype(BF16)
    p_all = p.reshape(depth * batch * seq, -1)
    for i in range(depth):
        j = i // 2
        if i % 2 == 0:
            ya, yb = _even_mixer(h, norm_mix[i], even_w_in[j], sgu_w[j], sgu_b[j], sgu_norm[j],
                                 batch, seq)
            w_out = even_w_out[j]
        else:
            ya, yb, w_out = _odd_mixer(h, norm_mix[i], odd_w_in[j], forget_bias[j], q_norm[j],
                                       k_norm[j], pool_w[j], pool_scale[j], odd_w_out[j], batch, seq)
        h = _moe_layer(i, h, ya, yb, w_out, norm_ffn[i], router_group_w[i], router_group_b[i],
                       router_expert_w[i], router_expert_b[i], w_gate_all, w_up_all, w_down_all,
                       p_all, ple_w_proj_all, ple_norm[i], ple_w_gate_all)
    return h.reshape(batch, seq, d)
```

```python
import functools

import jax
import jax.numpy as jnp
from jax import lax
from jax.experimental import pallas as pl
from jax.experimental.pallas import tpu as pltpu

F32 = jnp.float32
BF16 = jnp.bfloat16

HEAD_DIM = 64
LANES = 128
CHUNK = 128
POOL_WINDOWS = (2, 4, 8, 16)
MAX_WINDOW = max(POOL_WINDOWS)
N_GROUPS = 4
EXPERTS_PER_GROUP = 4
N_EXPERTS = N_GROUPS * EXPERTS_PER_GROUP
D_EXPERT = 256
RMS_EPS = 1e-6
ATT_BLOCK = 256
FORGET_Q_BLOCK = 512
FORGET_K_BLOCK = 512
LOG2_E = 1.4426950408889634
ROW_TILE = 512
MOE_TILE = 512
MOE_WINDOW = 176
MOE_SMALL_WINDOW = 112
BF16_ROWS = 16
VMEM_LIMIT = 48 * 1024 * 1024
MOE_VMEM_LIMIT = 56 * 1024 * 1024
EXP2_ZERO_BELOW = -150.0
NEG_BIG = -1e30


def _params(*sem):
    return pltpu.CompilerParams(dimension_semantics=sem, vmem_limit_bytes=VMEM_LIMIT)


def _split_bf16(x):
    hi = x.astype(BF16)
    lo = (x - hi.astype(F32)).astype(BF16)
    return hi, lo


def _dot(a, b):
    return jnp.dot(a, b, preferred_element_type=F32)


def _dot_nt(a, b):
    return lax.dot_general(a, b, (((1,), (1,)), ((), ())), preferred_element_type=F32)


def _dot_exact_rhs(x, m):
    hi, lo = _split_bf16(x)
    return _dot(hi, m) + _dot(lo, m)


def _rms_normalize(x):
    return x * lax.rsqrt(jnp.mean(x * x, axis=-1, keepdims=True) + RMS_EPS)


def _log_sigmoid(x):
    return jnp.minimum(x, 0.0) - jnp.log1p(jnp.exp(-jnp.abs(x)))


def _gelu_tanh(x):
    return 0.5 * x * (1.0 + jnp.tanh(0.7978845608028654 * (x + 0.044715 * (x * x * x))))


def _head_group_matrix():
    r = lax.broadcasted_iota(jnp.int32, (LANES, LANES), 0) // HEAD_DIM
    c = lax.broadcasted_iota(jnp.int32, (LANES, LANES), 1) // HEAD_DIM
    return jnp.where(r == c, 1.0, 0.0).astype(BF16)


def _first_head_mask(shape):
    return lax.broadcasted_iota(jnp.int32, shape, len(shape) - 1) < HEAD_DIM


def _even_in_kernel(h_ref, g_ref, w_ref, ws_ref, bs_ref, gv_ref, ya_ref, q_ref, k_ref, vb_ref, *, width):
    hn = (_rms_normalize(h_ref[...]) * g_ref[...]).astype(BF16)
    proj = lambda c: _dot(hn, w_ref[:, c * width:(c + 1) * width])
    n_pairs = width // LANES
    n_chunks = h_ref.shape[0] // CHUNK
    gmat = _head_group_matrix()
    v_all = _gelu_tanh(proj(1))
    u_all = _gelu_tanh(proj(0))
    q_ref[...] = (proj(2) * (HEAD_DIM ** -0.5 * LOG2_E)).astype(q_ref.dtype)
    ms_all = [_dot((v_all[:, p * LANES:(p + 1) * LANES] ** 2).astype(BF16), gmat) for p in range(n_pairs)]
    k_ref[...] = proj(3).astype(k_ref.dtype)
    vb_ref[...] = proj(4).astype(vb_ref.dtype)

    first = _first_head_mask((CHUNK, LANES))
    tril = (lax.broadcasted_iota(jnp.int32, (CHUNK, CHUNK), 0)
            >= lax.broadcasted_iota(jnp.int32, (CHUNK, CHUNK), 1))
    for p in range(n_pairs):
        cols = slice(p * LANES, (p + 1) * LANES)
        w_a = jnp.where(tril, ws_ref[2 * p], 0.0).astype(BF16)
        w_b = jnp.where(tril, ws_ref[2 * p + 1], 0.0).astype(BF16)
        v = v_all[:, cols]
        ms = ms_all[p] * (1.0 / HEAD_DIM)
        vn = v * lax.rsqrt(ms + RMS_EPS) * gv_ref[:, cols]
        for c in range(n_chunks):
            rows = slice(c * CHUNK, (c + 1) * CHUNK)
            vc = vn[rows]
            mixed = (_dot(w_a, jnp.where(first, vc, 0.0).astype(BF16))
                     + _dot(w_b, jnp.where(first, 0.0, vc).astype(BF16)) + bs_ref[:, cols])
            ya_ref[rows, cols] = (u_all[rows, cols] * mixed).astype(ya_ref.dtype)


def _even_in_proj(h, g, w, w_s, bias, g_v):
    m, d = h.shape
    width = w.shape[1] // 5
    out = jax.ShapeDtypeStruct((m, width), BF16)
    row = pl.BlockSpec((ROW_TILE, width), lambda i: (i, 0))
    full = lambda a: pl.BlockSpec(a.shape, lambda i: (0,) * a.ndim)
    return pl.pallas_call(
        functools.partial(_even_in_kernel, width=width),
        grid=(m // ROW_TILE,),
        in_specs=[pl.BlockSpec((ROW_TILE, d), lambda i: (i, 0)),
                  full(g), full(w), full(w_s), full(bias), full(g_v)],
        out_specs=[row] * 4,
        out_shape=[out] * 4,
        compiler_params=_params("parallel"),
        name="even_in_proj",
    )(h, g, w, w_s, bias, g_v)


def _stick_kernel(q_ref, k_ref, v_ref, o_ref, acc_ref, r_ref):
    blk = ATT_BLOCK
    n_pairs = q_ref.shape[1] // LANES
    i = pl.program_id(1)
    first_q = _first_head_mask((blk, LANES))
    row = lax.broadcasted_iota(jnp.int32, (blk, blk), 0)
    col = lax.broadcasted_iota(jnp.int32, (blk, blk), 1)
    suffix = jnp.where(row >= col, 1.0, 0.0).astype(BF16)
    suffix2 = jnp.concatenate([suffix, suffix], axis=0)
    strict = col < row
    acc_ref[...] = jnp.zeros_like(acc_ref)
    r_ref[...] = jnp.zeros_like(r_ref)

    def sweep(blocks):
        n_heads = 2 * n_pairs
        strict_all = jnp.concatenate([strict] * n_heads, axis=0)
        starts = [pl.multiple_of(j * blk, blk) for j, _ in blocks]
        q2s = []
        for p in range(n_pairs):
            q = q_ref[:, p * LANES:(p + 1) * LANES].astype(F32)
            q2s.append(jnp.concatenate([jnp.where(first_q, q, 0.0),
                                        jnp.where(first_q, 0.0, q)], axis=0).astype(BF16))
        zs = [jnp.concatenate([_dot_nt(q2s[p], k_ref[pl.ds(start, blk), p * LANES:(p + 1) * LANES])
                               for p in range(n_pairs)], axis=0) for start in starts]
        tails = []
        for z, (_, diagonal) in zip(zs, blocks):
            nz = -z
            log_1m_beta = jnp.minimum(nz, 0.0) - jnp.log2(1.0 + jnp.exp2(jnp.minimum(z, nz)))
            if diagonal:
                log_1m_beta = jnp.where(strict_all, log_1m_beta, 0.0)
            hi, lo = _split_bf16(log_1m_beta)
            tails.append(_dot(jnp.concatenate([hi, lo], axis=1), suffix2))
        r = r_ref[...]
        for z, tail, start, (_, diagonal) in zip(zs, tails, starts, blocks):
            weights = jnp.exp2(z + tail + jnp.concatenate([r] * (blk // LANES), axis=1))
            if diagonal:
                weights = jnp.where(strict_all, weights, 0.0)
            weights = weights.astype(BF16)
            r = r + jnp.broadcast_to(tail[:, 0:1], r.shape)
            for p in range(n_pairs):
                cols = slice(p * LANES, (p + 1) * LANES)
                v = v_ref[pl.ds(start, blk), cols].astype(F32)
                v2 = jnp.concatenate([jnp.where(first_q, v, 0.0), jnp.where(first_q, 0.0, v)], axis=0)
                w2 = jnp.concatenate([weights[2 * p * blk:(2 * p + 1) * blk],
                                      weights[(2 * p + 1) * blk:(2 * p + 2) * blk]], axis=1)
                acc_ref[p] += _dot(w2, v2.astype(BF16))
        r_ref[...] = r

    @pl.when(i == 0)
    def _():
        sweep([(i, True)])

    @pl.when(i > 0)
    def _():
        sweep([(i, True), (i - 1, False)])

    def cond(carry):
        j, live = carry
        return jnp.logical_and(j >= 0, live)

    def any_live():
        return jnp.max(r_ref[...]) > EXP2_ZERO_BELOW

    def body(carry):
        j, _ = carry
        sweep([(j, False)])
        return j - 1, any_live()

    lax.while_loop(cond, body, (i - 2, any_live()))
    for p in range(n_pairs):
        o_ref[:, p * LANES:(p + 1) * LANES] = acc_ref[p].astype(o_ref.dtype)


def _stick_breaking(q, k, v, batch, seq):
    m, width = q.shape
    nq = seq // ATT_BLOCK
    n_pairs = width // LANES
    qspec = pl.BlockSpec((ATT_BLOCK, width), lambda b, i: (b * nq + i, 0))
    kvspec = pl.BlockSpec((seq, width), lambda b, i: (b, 0))
    return pl.pallas_call(
        _stick_kernel,
        grid=(batch, nq),
        in_specs=[qspec, kvspec, kvspec],
        out_specs=qspec,
        out_shape=jax.ShapeDtypeStruct((m, width), BF16),
        scratch_shapes=[pltpu.VMEM((n_pairs, ATT_BLOCK, LANES), F32),
                        pltpu.VMEM((2 * n_pairs * ATT_BLOCK, LANES), F32)],
        compiler_params=_params("parallel", "arbitrary"),
        name="stick_breaking",
    )(q, k, v)


def _odd_in_kernel(h_ref, g_ref, w_ref, wf_ref, fb_ref, qn_ref, kn_ref, pw_ref, ps_ref,
                   q_ref, k_ref, v_ref, yd_ref, nf_ref, carry_ref, pad_ref, *, width, tiles_per_seq):
    i = pl.program_id(0)
    tile = h_ref.shape[0]

    @pl.when(i % tiles_per_seq == 0)
    def _():
        carry_ref[...] = jnp.zeros_like(carry_ref)
        pad_ref[0:MAX_WINDOW, :] = jnp.zeros((MAX_WINDOW, width), F32)

    hn32 = _rms_normalize(h_ref[...]) * g_ref[...]
    hn, hn_lo = _split_bf16(hn32)
    gmat = _head_group_matrix()

    def head_norm(x, gain_ref):
        parts = []
        for p in range(width // LANES):
            xp = x[:, p * LANES:(p + 1) * LANES]
            ms = _dot((xp * xp).astype(BF16), gmat) * (1.0 / HEAD_DIM)
            parts.append(xp * lax.rsqrt(ms + RMS_EPS) * gain_ref[...])
        return jnp.concatenate(parts, axis=1)

    n_heads = fb_ref.shape[0]
    f_t = jnp.transpose(_dot(hn, wf_ref[...]) + _dot(hn_lo, wf_ref[...]))
    f_logit = f_t[0:n_heads] + f_t[n_heads:2 * n_heads]

    pad_ref[MAX_WINDOW:, :] = _dot(hn, w_ref[:, 3 * width:4 * width])
    pos = (i % tiles_per_seq) * tile + lax.broadcasted_iota(jnp.int32, (tile, LANES), 0)
    pooled = []
    for gi, win in enumerate(POOL_WINDOWS):
        cols = slice(gi * LANES, (gi + 1) * LANES)
        x = pad_ref[MAX_WINDOW:, cols]
        total = x
        for back in range(1, win):
            total = total + pad_ref[MAX_WINDOW - back:MAX_WINDOW - back + tile, cols]
        pooled.append((total / jnp.minimum(pos + 1, win).astype(F32) - x).astype(BF16))
    pad_ref[0:MAX_WINDOW, :] = pad_ref[tile:tile + MAX_WINDOW, :]

    qf = head_norm(_dot(hn, w_ref[:, 0:width]), qn_ref) * (HEAD_DIM ** -0.5 * LOG2_E)
    q_ref[...] = qf.astype(q_ref.dtype)

    log_f = _log_sigmoid(f_logit + fb_ref[...])
    n_heads, tile = log_f.shape
    r = lax.broadcasted_iota(jnp.int32, (LANES, LANES), 0)
    c = lax.broadcasted_iota(jnp.int32, (LANES, LANES), 1)
    prefix = jnp.where(r <= c, 1.0, 0.0).astype(BF16)
    local = []
    for s in range(tile // LANES):
        x = log_f[:, s * LANES:(s + 1) * LANES]
        x1 = x.astype(BF16)
        rem = x - x1.astype(F32)
        x2 = rem.astype(BF16)
        x3 = (rem - x2.astype(F32)).astype(BF16)
        local.append(_dot(x1, prefix) + (_dot(x2, prefix) + _dot(x3, prefix)))

    k_ref[...] = head_norm(_dot(hn, w_ref[:, width:2 * width]), kn_ref).astype(k_ref.dtype)

    carry = carry_ref[...]
    for s in range(tile // LANES):
        cum = local[s] + carry
        nf_ref[:, s * LANES:(s + 1) * LANES] = cum * (-LOG2_E)
        carry = jnp.broadcast_to(cum[:, LANES - 1:LANES], carry.shape)
    carry_ref[...] = carry

    v_ref[...] = _dot(hn, w_ref[:, 2 * width:3 * width]).astype(v_ref.dtype)

    for gi in range(len(POOL_WINDOWS)):
        cols = slice(gi * LANES, (gi + 1) * LANES)
        mixed = _dot(pooled[gi], pw_ref[gi].astype(BF16)) * ps_ref[:, cols]
        yd_ref[:, cols] = mixed.astype(yd_ref.dtype)


def _odd_in_proj(h, g, w, wf, fb, qn, kn, pool_w, pool_scale, seq):
    m, d = h.shape
    width = w.shape[1] // 4
    n_heads = fb.shape[0]
    row = pl.BlockSpec((ROW_TILE, width), lambda i: (i, 0))
    full = lambda a: pl.BlockSpec(a.shape, lambda i: (0,) * a.ndim)
    return pl.pallas_call(
        functools.partial(_odd_in_kernel, width=width, tiles_per_seq=seq // ROW_TILE),
        grid=(m // ROW_TILE,),
        in_specs=[pl.BlockSpec((ROW_TILE, d), lambda i: (i, 0)),
                  full(g), full(w), full(wf), full(fb), full(qn), full(kn), full(pool_w),
                  full(pool_scale)],
        out_specs=[row, row, row, row, pl.BlockSpec((n_heads, ROW_TILE), lambda i: (0, i))],
        out_shape=[jax.ShapeDtypeStruct((m, width), BF16)] * 4
                  + [jax.ShapeDtypeStruct((n_heads, m), F32)],
        scratch_shapes=[pltpu.VMEM((n_heads, LANES), F32),
                        pltpu.VMEM((MAX_WINDOW + ROW_TILE, width), F32)],
        compiler_params=_params("arbitrary"),
        name="odd_in_proj",
    )(h, g, w, wf, fb, qn, kn, pool_w, pool_scale)


def _forget_kernel(qk_bound_ref, q_ref, k_ref, v_ref, nf_ref, o_ref, acc_ref, m_ref):
    tq, tk = FORGET_Q_BLOCK, FORGET_K_BLOCK
    i = pl.program_id(2)
    first_q = _first_head_mask((tq, LANES))
    lane = lax.broadcasted_iota(jnp.int32, (tk, LANES), 1)
    first_k = lane < HEAD_DIM
    q = q_ref[...].astype(F32)
    q2 = jnp.concatenate([jnp.where(first_q, q, 0.0), jnp.where(first_q, 0.0, q)], axis=0).astype(BF16)
    den_lane = (HEAD_DIM, 0)
    den_cols = [jnp.where(lane == dl, 1.0, 0.0) for dl in den_lane]
    acc_ref[...] = jnp.zeros_like(acc_ref)
    m_ref[...] = jnp.full_like(m_ref, NEG_BIG)
    diag_block = (i * tq) // tk

    def sweep(j, diagonal, n_blocks=1):
        span = n_blocks * tk
        start = pl.multiple_of(j * tk, tk)
        k = k_ref[pl.ds(start, span), :]
        v = v_ref[pl.ds(start, span), :].astype(F32)
        first_s = jnp.concatenate([first_k] * n_blocks, axis=0)
        v_heads = (jnp.where(first_s, v, jnp.concatenate([den_cols[0]] * n_blocks, axis=0)).astype(BF16),
                   jnp.where(first_s, jnp.concatenate([den_cols[1]] * n_blocks, axis=0), v).astype(BF16))
        if diagonal:
            q_pos = i * tq + lax.broadcasted_iota(jnp.int32, (tq, span), 0)
            k_pos = start + lax.broadcasted_iota(jnp.int32, (tq, span), 1)
            causal = k_pos <= q_pos
        s2 = _dot_nt(q2, k)
        s_heads = [s2[hd * tq:(hd + 1) * tq] + nf_ref[hd:hd + 1, pl.ds(start, span)] for hd in range(2)]
        if diagonal:
            s_heads = [jnp.where(causal, s, NEG_BIG) for s in s_heads]
        s = jnp.concatenate(s_heads, axis=0)
        m_prev = m_ref[...]
        m_new = jnp.maximum(m_prev, jnp.broadcast_to(jnp.max(s, axis=1, keepdims=True), m_prev.shape))
        p = jnp.exp2(s - jnp.concatenate([m_new] * (span // LANES), axis=1)).astype(BF16)
        pv = jnp.concatenate([_dot(p[hd * tq:(hd + 1) * tq], v_heads[hd]) for hd in range(2)], axis=0)
        acc_ref[...] = jnp.exp2(m_prev - m_new) * acc_ref[...] + pv
        m_ref[...] = m_new

    def block_live(j):
        tail = pl.multiple_of(jnp.maximum(j, 0) * tk + tk - LANES, LANES)
        live = False
        for hd in range(2):
            top = qk_bound_ref[0, 0] + jnp.max(nf_ref[hd:hd + 1, pl.ds(tail, LANES)], axis=1, keepdims=True)
            live = jnp.logical_or(live, jnp.max(top - m_ref[hd * tq:(hd + 1) * tq, :]) > EXP2_ZERO_BELOW)
        return live

    sweep(diag_block, True)

    def cond(carry):
        j, live = carry
        return jnp.logical_and(j >= 1, live)

    def body(carry):
        j, _ = carry
        sweep(j - 1, False, n_blocks=2)
        return j - 2, block_live(j - 2)

    j_end, live_end = lax.while_loop(cond, body, (diag_block - 1, block_live(diag_block - 1)))

    @pl.when(jnp.logical_and(j_end == 0, live_end))
    def _():
        sweep(0, False)

    acc_a = acc_ref[0:tq, :]
    acc_b = acc_ref[tq:, :]
    den_a = jnp.broadcast_to(acc_a[:, den_lane[0]:den_lane[0] + 1], acc_a.shape)
    den_b = jnp.broadcast_to(acc_b[:, den_lane[1]:den_lane[1] + 1], acc_b.shape)
    o_ref[...] = jnp.where(first_q, acc_a / den_a, acc_b / den_b).astype(o_ref.dtype)


def _forgetting_attention(qk_bound, q, k, v, neg_f, batch, seq):
    m, width = q.shape
    tq = FORGET_Q_BLOCK
    assert FORGET_K_BLOCK % tq == 0 and seq % FORGET_K_BLOCK == 0
    nq = seq // tq
    n_pairs = width // LANES
    neg_f = neg_f.reshape(n_pairs, 2, m)
    qspec = pl.BlockSpec((tq, LANES), lambda b, p, i: (b * nq + i, p))
    kvspec = pl.BlockSpec((seq, LANES), lambda b, p, i: (b, p))
    return pl.pallas_call(
        _forget_kernel,
        grid=(batch, n_pairs, nq),
        in_specs=[pl.BlockSpec(memory_space=pltpu.SMEM), qspec, kvspec, kvspec,
                  pl.BlockSpec((None, 2, seq), lambda b, p, i: (p, 0, b))],
        out_specs=qspec,
        out_shape=jax.ShapeDtypeStruct((m, width), BF16),
        scratch_shapes=[pltpu.VMEM((2 * tq, LANES), F32),
                        pltpu.VMEM((2 * tq, LANES), F32)],
        compiler_params=_params("parallel", "parallel", "arbitrary"),
        name="forgetting_attention",
    )(qk_bound, q, k, v, neg_f)


def _route(logits):
    lane = lax.broadcasted_iota(jnp.int32, logits.shape, 1)
    is_group = lane < N_GROUPS
    gl = jnp.where(is_group, logits, NEG_BIG)
    g_max = jnp.max(gl, axis=1, keepdims=True)
    g_idx = jnp.min(jnp.where(gl == g_max, lane, LANES), axis=1, keepdims=True)
    g_den = jnp.sum(jnp.where(is_group, jnp.exp(gl - g_max), 0.0), axis=1, keepdims=True)
    g_p = 1.0 / g_den
    lo = N_GROUPS + g_idx * EXPERTS_PER_GROUP
    in_group = jnp.logical_and(lane >= lo, lane < lo + EXPERTS_PER_GROUP)
    el = jnp.where(in_group, logits, NEG_BIG)
    top1 = jnp.max(el, axis=1, keepdims=True)
    idx1 = jnp.min(jnp.where(el == top1, lane, LANES), axis=1, keepdims=True)
    el2 = jnp.where(lane == idx1, NEG_BIG, el)
    top2 = jnp.max(el2, axis=1, keepdims=True)
    idx2 = jnp.min(jnp.where(el2 == top2, lane, LANES), axis=1, keepdims=True)
    e2 = jnp.exp(top2 - top1)
    den = 1.0 + e2
    w1 = (1.0 / den) * g_p
    w2 = (e2 / den) * g_p
    return jnp.where(lane == idx1, w1, 0.0) + jnp.where(lane == idx2, w2, 0.0), g_idx


def _moe_kernel(h_ref, ya_ref, yb_ref, wo_ref, g_ref, wr_ref, br_ref,
                wgate_ref, wup_ref, wd_ref, p_ref, wp_ref, gn_ref, wg_ref, o_ref,
                h1_ref, emb_ref, xs_ref, gs_ref, ys_ref, pos_ref, cnt_ref):
    t, d = h_ref.shape
    win = MOE_WINDOW

    def sort_by_group():
        half = ya_ref.shape[1]
        h1 = h_ref[...] + _dot(ya_ref[...], wo_ref[0:half, :]) + _dot(yb_ref[...], wo_ref[half:, :])
        h1_ref[...] = h1
        hn = _rms_normalize(h1) * g_ref[...]
        hn_hi, hn_lo = _split_bf16(hn)
        both = _dot(hn_hi, wr_ref[...])
        logits = both[:, :LANES] + (both[:, LANES:] + _dot(hn_lo, wr_ref[:, :LANES])) + br_ref[...]
        emb_ref[...] = _rms_normalize(_dot(p_ref[...].astype(BF16), wp_ref[...])) * gn_ref[...]
        gates, g_idx = _route(logits)
        lane = lax.broadcasted_iota(jnp.int32, (t, LANES), 1)
        onehot = jnp.where(lane == g_idx, 1.0, 0.0)
        r_i = lax.broadcasted_iota(jnp.int32, (t, t), 0)
        c_i = lax.broadcasted_iota(jnp.int32, (t, t), 1)
        earlier = jnp.where(c_i < r_i, 1.0, 0.0).astype(BF16)
        rank = _dot(earlier, onehot.astype(BF16))
        counts = jnp.broadcast_to(rank[t - 1:t, :] + onehot[t - 1:t, :], (8, LANES))
        lr = lax.broadcasted_iota(jnp.int32, (LANES, LANES), 0)
        lc = lax.broadcasted_iota(jnp.int32, (LANES, LANES), 1)
        offsets = _dot_exact_rhs(counts, jnp.where(lr < lc, 1.0, 0.0).astype(BF16))
        pos = jnp.sum(onehot * (offsets[0:1, :] + rank), axis=1, keepdims=True)
        pos_b = jnp.broadcast_to(pos, (t, LANES)).astype(jnp.int32)
        pos_ref[...] = pos_b
        cnt_ref[0:8, :] = counts.astype(jnp.int32)
        cnt_ref[8:16, :] = offsets.astype(jnp.int32)
        pos_row = jnp.transpose(pos_b)[0:1, :]
        perm = jnp.where(r_i == pos_row, 1.0, 0.0).astype(BF16)
        g_hi, g_lo = _split_bf16(gates)
        srt = _dot(perm, jnp.concatenate([hn_hi, g_hi, g_lo], axis=1))
        xs_ref[0:t, :] = srt[:, :d].astype(BF16)
        xs_ref[t:, :] = jnp.zeros((win, d), BF16)
        gs_ref[0:t, :] = srt[:, d:d + LANES] + srt[:, d + LANES:]
        gs_ref[t:, :] = jnp.zeros((win, LANES), F32)
        ys_ref[...] = jnp.zeros_like(ys_ref)

    def run_group(grp, carry):
        lane1 = lax.broadcasted_iota(jnp.int32, (1, LANES), 1)
        count = jnp.sum(jnp.where(lane1 == grp, cnt_ref[0:1, :], 0))
        start = jnp.sum(jnp.where(lane1 == grp, cnt_ref[8:9, :], 0))
        base = (start // BF16_ROWS) * BF16_ROWS
        rows = start - base + count
        n_full = rows // win
        rest = rows - n_full * win
        n_small = jnp.logical_and(rest > 0, rest <= MOE_SMALL_WINDOW).astype(jnp.int32)
        n_full = n_full + (rest > MOE_SMALL_WINDOW).astype(jnp.int32)

        def window(r0, size):
            r0 = pl.multiple_of(r0, BF16_ROWS)
            xs = xs_ref[pl.ds(r0, size), :]
            gs = gs_ref[pl.ds(r0, size), :]
            lane = lax.broadcasted_iota(jnp.int32, (size, LANES), 1)
            parts = []
            for e in range(EXPERTS_PER_GROUP):
                expert = grp * EXPERTS_PER_GROUP + e
                hg = _dot(xs, wgate_ref[expert])
                hu = _dot(xs, wup_ref[expert])
                gate = jnp.sum(jnp.where(lane == N_GROUPS + expert, gs, 0.0), axis=1, keepdims=True)
                parts.append(((hg * jax.nn.sigmoid(hg)) * hu * gate).astype(BF16))
            ys_ref[pl.ds(r0, size), :] += _dot(jnp.concatenate(parts, axis=1), wd_ref[grp])

        def full_window(w, c):
            window(base + w * win, win)
            return c

        def small_window(w, c):
            window(base + n_full * win, MOE_SMALL_WINDOW)
            return c

        carry = lax.fori_loop(0, n_full, full_window, carry)
        return lax.fori_loop(0, n_small, small_window, carry)

    def unsort_and_embed():
        c_i = lax.broadcasted_iota(jnp.int32, (t, t), 1)
        pos_t = jnp.concatenate([pos_ref[...]] * (t // LANES), axis=1)
        unperm = jnp.where(c_i == pos_t, 1.0, 0.0).astype(BF16)
        h2 = h1_ref[...] + _dot(unperm, ys_ref[0:t, :].astype(BF16))
        emb_gate = jax.nn.sigmoid(_dot(_rms_normalize(h2).astype(BF16), wg_ref[...]))
        o_ref[...] = h2 + emb_gate * emb_ref[...]

    sort_by_group()
    lax.fori_loop(0, N_GROUPS, run_group, 0)
    unsort_and_embed()


def _moe(layer, h, ya, yb, wo, g, wr, br, wgate, wup, wd, p, wp, gn, wg):
    m, d = h.shape
    n_tiles = m // MOE_TILE
    rows = lambda a: pl.BlockSpec((MOE_TILE, a.shape[1]), lambda i: (i, 0))
    full = lambda a: pl.BlockSpec(a.shape, lambda i: (0,) * a.ndim, pipeline_mode=pl.Buffered(1))
    of_layer = lambda a: pl.BlockSpec((None,) + a.shape[1:], lambda i: (layer,) + (0,) * (a.ndim - 1),
                                      pipeline_mode=pl.Buffered(1))
    p_rows = pl.BlockSpec((MOE_TILE, p.shape[1]), lambda i: (layer * n_tiles + i, 0))
    return pl.pallas_call(
        _moe_kernel,
        grid=(n_tiles,),
        in_specs=[rows(h), rows(ya), rows(yb), full(wo), full(g), full(wr), full(br),
                  of_layer(wgate), of_layer(wup), of_layer(wd), p_rows, of_layer(wp), full(gn),
                  of_layer(wg)],
        out_specs=rows(h),
        out_shape=jax.ShapeDtypeStruct((m, d), F32),
        scratch_shapes=[pltpu.VMEM((MOE_TILE, d), F32),
                        pltpu.VMEM((MOE_TILE, d), F32),
                        pltpu.VMEM((MOE_TILE + MOE_WINDOW, d), BF16),
                        pltpu.VMEM((MOE_TILE + MOE_WINDOW, LANES), F32),
                        pltpu.VMEM((MOE_TILE + MOE_WINDOW, d), F32),
                        pltpu.VMEM((MOE_TILE, LANES), jnp.int32),
                        pltpu.VMEM((16, LANES), jnp.int32)],
        compiler_params=pltpu.CompilerParams(dimension_semantics=("parallel",),
                                             vmem_limit_bytes=MOE_VMEM_LIMIT),
        name="outproj_moe_ple",
    )(h, ya, yb, wo, g, wr, br, wgate, wup, wd, p, wp, gn, wg)


def _row(a):
    return a.reshape(1, -1).astype(F32)


def _hi_lo(w):
    hi = w.astype(BF16)
    return hi, (w - hi.astype(F32)).astype(BF16)


def _even_mixer(h, g, w_in, sgu_w, sgu_b, sgu_norm, batch, seq):
    bias = jnp.repeat(sgu_b.T, HEAD_DIM, axis=1)
    ya, q, k, vb = _even_in_proj(h, _row(g), w_in.astype(BF16), sgu_w, bias, _row(sgu_norm))
    yb = _stick_breaking(q, k, vb, batch, seq)
    return ya, yb


def _odd_mixer(h, g, w_in, forget_bias, q_norm, k_norm, pool_w, pool_scale, w_out, batch, seq):
    n_heads = forget_bias.shape[0]
    width = n_heads * HEAD_DIM
    order = jnp.argsort(forget_bias)
    cols = (order[:, None] * HEAD_DIM + jnp.arange(HEAD_DIM)[None, :]).reshape(-1)
    w_main = jnp.concatenate([w_in[:, c * width:(c + 1) * width][:, cols] for c in range(3)]
                             + [w_in[:, 3 * width + n_heads:]], axis=1).astype(BF16)
    wf_hi, wf_lo = _hi_lo(w_in[:, 3 * width:3 * width + n_heads][:, order])
    wf = jnp.zeros((w_in.shape[0], LANES), BF16).at[:, :n_heads].set(wf_hi).at[:, n_heads:2 * n_heads].set(wf_lo)
    forget_bias = forget_bias[order]
    w_out = jnp.concatenate([w_out[:width][cols], w_out[width:]], axis=0)
    gain = lambda t: jnp.tile(t, LANES // HEAD_DIM).reshape(1, LANES).astype(F32)
    q, k, v, yd, neg_f = _odd_in_proj(h, _row(g), w_main, wf,
                                      forget_bias.reshape(n_heads, 1).astype(F32),
                                      gain(q_norm), gain(k_norm), pool_w, _row(pool_scale), seq)
    qk_bound = (1.02 * HEAD_DIM * HEAD_DIM ** -0.5 * LOG2_E
                * jnp.max(jnp.abs(q_norm)) * jnp.max(jnp.abs(k_norm))).reshape(1, 1).astype(F32)
    yc = _forgetting_attention(qk_bound, q, k, v, neg_f, batch, seq)
    return yc, yd, w_out


def _moe_layer(layer, h, ya, yb, w_out, g, rg_w, rg_b, re_w, re_b, w_gate, w_up, w_down,
               p, ple_w_proj, ple_norm, ple_w_gate):
    d = h.shape[1]
    n_route = N_GROUPS + N_EXPERTS
    wr = jnp.zeros((d, LANES), F32).at[:, :N_GROUPS].set(rg_w).at[:, N_GROUPS:n_route].set(re_w)
    br = jnp.zeros((1, LANES), F32).at[0, :N_GROUPS].set(rg_b).at[0, N_GROUPS:n_route].set(re_b)
    wr = jnp.concatenate(_hi_lo(wr), axis=1)
    return _moe(layer, h, ya, yb, w_out.astype(BF16), _row(g), wr, br,
                w_gate, w_up, w_down, p, ple_w_proj, _row(ple_norm), ple_w_gate)


def kernel(x, p, norm_mix, norm_ffn, even_w_in, sgu_w, sgu_b, sgu_norm, even_w_out, odd_w_in, forget_bias, q_norm, k_norm, pool_w, pool_scale, odd_w_out, router_group_w, router_group_b, router_expert_w, router_expert_b, expert_w_gate, expert_w_up, expert_w_down, ple_w_proj, ple_norm, ple_w_gate):
    batch, seq, d = x.shape
    depth = p.shape[0]
    h = x.reshape(batch * seq, d)
    w_gate_all = expert_w_gate.astype(BF16)
    w_up_all = expert_w_up.astype(BF16)
    w_down_all = expert_w_down.reshape(depth, N_GROUPS, EXPERTS_PER_GROUP * D_EXPERT, d).astype(BF16)
    ple_w_proj_all = ple_w_proj.astype(BF16)
    ple_w_gate_all = ple_w_gate.astype(BF16)
    p_all = p.reshape(depth * batch * seq, -1)
    for i in range(depth):
        j = i // 2
        if i % 2 == 0:
            ya, yb = _even_mixer(h, norm_mix[i], even_w_in[j], sgu_w[j], sgu_b[j], sgu_norm[j],
                                 batch, seq)
            w_out = even_w_out[j]
        else:
            ya, yb, w_out = _odd_mixer(h, norm_mix[i], odd_w_in[j], forget_bias[j], q_norm[j],
                                       k_norm[j], pool_w[j], pool_scale[j], odd_w_out[j], batch, seq)
        h = _moe_layer(i, h, ya, yb, w_out, norm_ffn[i], router_group_w[i], router_group_b[i],
                       router_expert_w[i], router_expert_b[i], w_gate_all, w_up_all, w_down_all,
                       p_all, ple_w_proj_all, ple_norm[i], ple_w_gate_all)
    return h.reshape(batch, seq, d)
```
